```python
import jax, jax.numpy as jnp
from jax import lax
import numpy as np

D_MODEL = 2048
BATCH = 8
SEQ = 4096
DEPTH = 2
DEC_BATCH = 1
DEC_SEQ = 8192
PAST_LEN = 128

GRID_W = 64
ATTN_HEADS = 16
HEAD_DIM = 64
ATTN_W = ATTN_HEADS * HEAD_DIM
NA_ROWS = 8
NA_COLS = 16
LRU_W = D_MODEL - ATTN_W
LRU_BLOCKS = 8
LRU_BLOCK_W = LRU_W // LRU_BLOCKS
CONV_W = 4
RG_C = 8.0
IN_W = 3 * ATTN_W + 2 * LRU_W
N_EXPERTS = 32
TOP_K = 4
D_FF = D_MODEL
SWIGLU_LIMIT = 7.0
SWIGLU_ALPHA = 1.702
MOE_BLOCK = 256
NORM_EPS = 1e-5

kernel_name = "hymba_na_rglru_moe_encoder"

F32 = jnp.float32


def rms_norm(x, g):
    xf = x.astype(F32)
    y = xf * lax.rsqrt(jnp.mean(xf * xf, axis=-1, keepdims=True) + NORM_EPS)
    return (y * g.astype(F32)).astype(x.dtype)


def neighbourhood_attention(q, k, v, rpb):
    B, T, H, Dh = q.shape
    rows = T // GRID_W
    kh = min(NA_ROWS, rows)
    shp = (B, rows, GRID_W, H, Dh)
    qg, kg, vg = q.reshape(shp), k.reshape(shp), v.reshape(shp)
    col = jnp.arange(GRID_W)
    col_start = jnp.clip(col - NA_COLS // 2, 0, GRID_W - NA_COLS)
    col_idx = col_start[:, None] + jnp.arange(NA_COLS)[None, :]
    dc_idx = col_idx - col[:, None] + (NA_COLS - 1)
    scale = Dh ** -0.5

    def one_row(r):
        r0 = jnp.clip(r - kh // 2, 0, rows - kh)
        k_win = lax.dynamic_slice_in_dim(kg, r0, kh, axis=1)[:, :, col_idx]
        v_win = lax.dynamic_slice_in_dim(vg, r0, kh, axis=1)[:, :, col_idx]
        q_row = lax.dynamic_index_in_dim(qg, r, axis=1, keepdims=False)
        dr_idx = r0 + jnp.arange(kh) - r + (NA_ROWS - 1)
        bias = rpb[:, dr_idx[None, :, None], dc_idx[:, None, :]]
        s = jnp.einsum('bqhd,bkqjhd->bhqkj', q_row, k_win).astype(F32) * scale + bias.astype(F32)
        p = jax.nn.softmax(s.reshape(B, H, GRID_W, kh * NA_COLS), axis=-1)
        p = p.reshape(s.shape).astype(v.dtype)
        return jnp.einsum('bhqkj,bkqjhd->bqhd', p, v_win)

    out = lax.map(one_row, jnp.arange(rows))
    return jnp.moveaxis(out, 0, 1).reshape(B, T, H * Dh)


def _linear_combine(e1, e2):
    a1, b1 = e1
    a2, b2 = e2
    return a1 * a2, a2 * b1 + b2


def rglru_direction(u, w_a, b_a, w_x, b_x, lam, reverse):
    B, T, C = u.shape
    uf = u.astype(F32)
    ub = uf.reshape(B, T, LRU_BLOCKS, LRU_BLOCK_W)
    r = jax.nn.sigmoid(jnp.einsum('btni,nio->btno', ub, w_a.astype(F32)) + b_a.astype(F32)).reshape(B, T, C)
    i = jax.nn.sigmoid(jnp.einsum('btni,nio->btno', ub, w_x.astype(F32)) + b_x.astype(F32)).reshape(B, T, C)
    log_a = -RG_C * r * jax.nn.softplus(-lam.astype(F32))
    a = jnp.exp(log_a)
    mult = jnp.sqrt(-jnp.expm1(2.0 * log_a))
    first = T - 1 if reverse else 0
    mult = jnp.where((jnp.arange(T) == first)[None, :, None], 1.0, mult)
    b = mult * (i * uf)
    _, h = lax.associative_scan(_linear_combine, (a, b), axis=1, reverse=reverse)
    return h


def rglru_bidirectional(u, conv_w, conv_b, w_a, b_a, w_x, b_x, lam):
    pad = (CONV_W // 2, CONV_W - 1 - CONV_W // 2)
    c = lax.conv_general_dilated(u, conv_w.reshape(CONV_W, 1, LRU_W).astype(u.dtype), (1,), [pad],
                                 dimension_numbers=('NWC', 'WIO', 'NWC'),
                                 feature_group_count=LRU_W) + conv_b
    h_f = rglru_direction(c, w_a[0], b_a[0], w_x[0], b_x[0], lam[0], reverse=False)
    h_b = rglru_direction(c, w_a[1], b_a[1], w_x[1], b_x[1], lam[1], reverse=True)
    return h_f + h_b


def clamped_swiglu(hgu):
    gate = jnp.minimum(hgu[..., ::2], SWIGLU_LIMIT)
    up = jnp.clip(hgu[..., 1::2], -SWIGLU_LIMIT, SWIGLU_LIMIT)
    return (up + 1.0) * (gate * jax.nn.sigmoid(SWIGLU_ALPHA * gate))


def moe_ffn(h, w_router, b_router, w_gu, b_gu, w_dn, b_dn):
    B, T, D = h.shape
    xt = h.reshape(-1, D)
    N = xt.shape[0]
    logits = xt.astype(F32) @ w_router.astype(F32) + b_router.astype(F32)
    top_val, top_idx = lax.top_k(logits, TOP_K)
    gates = jax.nn.softmax(top_val, axis=-1)
    M = N * TOP_K
    flat_e = top_idx.reshape(-1).astype(jnp.int32)
    flat_tok = jnp.arange(M, dtype=jnp.int32) // TOP_K
    flat_w = gates.reshape(-1)
    order = jnp.argsort(flat_e)
    sorted_e = flat_e[order]
    counts = jnp.bincount(flat_e, length=N_EXPERTS).astype(jnp.int32)
    padded = (counts + MOE_BLOCK - 1) // MOE_BLOCK * MOE_BLOCK
    start = jnp.cumsum(counts) - counts
    pend = jnp.cumsum(padded)
    pstart = pend - padded
    dest = pstart[sorted_e] + jnp.arange(M, dtype=jnp.int32) - start[sorted_e]
    n_blocks = -(-M // MOE_BLOCK) + N_EXPERTS
    P = n_blocks * MOE_BLOCK
    row_tok = jnp.zeros((P,), jnp.int32).at[dest].set(flat_tok[order])
    row_w = jnp.zeros((P,), F32).at[dest].set(flat_w[order])
    block_e = jnp.minimum(jnp.searchsorted(pend, jnp.arange(n_blocks, dtype=jnp.int32) * MOE_BLOCK,
                                           side='right'), N_EXPERTS - 1).astype(jnp.int32)

    def body(acc, blk):
        tok, w, e = blk
        xb = xt[tok]
        hgu = xb @ w_gu[e] + b_gu[e]
        y = clamped_swiglu(hgu) @ w_dn[e] + b_dn[e]
        return acc.at[tok].add(y * w[:, None].astype(y.dtype)), None

    out, _ = lax.scan(body, jnp.zeros_like(xt),
                      (row_tok.reshape(n_blocks, MOE_BLOCK), row_w.reshape(n_blocks, MOE_BLOCK), block_e))
    return out.reshape(B, T, D)


def mixer(x, g_mix, w_in, rpb, conv_w, conv_b, w_a, b_a, w_x, b_x, lam, g_attn, g_lru, w_out):
    B, T, _ = x.shape
    h = rms_norm(x, g_mix)
    z = h @ w_in
    q, k, v, u, g = jnp.split(z, [ATTN_W, 2 * ATTN_W, 3 * ATTN_W, 3 * ATTN_W + LRU_W], axis=-1)
    hd = lambda t: t.reshape(B, T, ATTN_HEADS, HEAD_DIM)
    attn = neighbourhood_attention(hd(q), hd(k), hd(v), rpb)
    rec = rglru_bidirectional(u, conv_w, conv_b, w_a, b_a, w_x, b_x, lam)
    rec = (rec * jax.nn.gelu(g.astype(F32))).astype(x.dtype)
    mixed = jnp.concatenate([rms_norm(attn, g_attn), rms_norm(rec, g_lru)], axis=-1)
    return mixed @ w_out


def trunk(x, g_mix, w_in, rpb, conv_w, conv_b, w_rg_a, b_rg_a, w_rg_x, b_rg_x, rg_lambda,
          g_attn_out, g_lru_out, w_out, g_ffn, w_router, b_router, w_gate_up, b_gate_up,
          w_down, b_down, g_final):
    for l in range(DEPTH):
        x = x + mixer(x, g_mix[l], w_in[l], rpb[l], conv_w[l], conv_b[l], w_rg_a[l], b_rg_a[l],
                      w_rg_x[l], b_rg_x[l], rg_lambda[l], g_attn_out[l], g_lru_out[l], w_out[l])
        x = x + moe_ffn(rms_norm(x, g_ffn[l]), w_router[l], b_router[l], w_gate_up[l],
                        b_gate_up[l], w_down[l], b_down[l])
    return rms_norm(x, g_final)


def setup_inputs(seed: int = 0) -> dict:
    key = jax.random.key(seed)
    ks = jax.random.split(key, 24)
    nrm = lambda k, shape, scale: scale * jax.random.normal(k, shape, F32)
    gain = lambda k, shape: 1.0 + 0.05 * jax.random.normal(k, shape, F32)
    a0 = jax.random.uniform(ks[10], (DEPTH, 2, LRU_W), F32, 0.9, 0.999)
    s = a0 ** (1.0 / RG_C)
    rg_lambda = jnp.log(s) - jnp.log1p(-s)
    return {
        "x_prompt": jax.random.normal(ks[0], (BATCH, SEQ, D_MODEL), F32),
        "x_sample": jax.random.normal(ks[1], (DEC_BATCH, DEC_SEQ, D_MODEL), F32),
        "g_mix": gain(ks[2], (DEPTH, D_MODEL)),
        "w_in": nrm(ks[3], (DEPTH, D_MODEL, IN_W), D_MODEL ** -0.5),
        "rpb": nrm(ks[4], (DEPTH, ATTN_HEADS, 2 * NA_ROWS - 1, 2 * NA_COLS - 1), 0.1),
        "conv_w": nrm(ks[5], (DEPTH, CONV_W, LRU_W), CONV_W ** -0.5),
        "conv_b": nrm(ks[6], (DEPTH, LRU_W), 0.01),
        "w_rg_a": nrm(ks[7], (DEPTH, 2, LRU_BLOCKS, LRU_BLOCK_W, LRU_BLOCK_W), LRU_BLOCK_W ** -0.5),
        "b_rg_a": nrm(ks[8], (DEPTH, 2, LRU_BLOCKS, LRU_BLOCK_W), 0.01),
        "w_rg_x": nrm(ks[9], (DEPTH, 2, LRU_BLOCKS, LRU_BLOCK_W, LRU_BLOCK_W), LRU_BLOCK_W ** -0.5),
        "b_rg_x": nrm(ks[11], (DEPTH, 2, LRU_BLOCKS, LRU_BLOCK_W), 0.01),
        "rg_lambda": rg_lambda,
        "g_attn_out": gain(ks[12], (DEPTH, ATTN_W)),
        "g_lru_out": gain(ks[13], (DEPTH, LRU_W)),
        "w_out": nrm(ks[14], (DEPTH, D_MODEL, D_MODEL), D_MODEL ** -0.5),
        "g_ffn": gain(ks[15], (DEPTH, D_MODEL)),
        "w_router": nrm(ks[16], (DEPTH, D_MODEL, N_EXPERTS), D_MODEL ** -0.5),
        "b_router": nrm(ks[17], (DEPTH, N_EXPERTS), 0.01),
        "w_gate_up": nrm(ks[18], (DEPTH, N_EXPERTS, D_MODEL, 2 * D_FF), D_MODEL ** -0.5),
        "b_gate_up": nrm(ks[19], (DEPTH, N_EXPERTS, 2 * D_FF), 0.01),
        "w_down": nrm(ks[20], (DEPTH, N_EXPERTS, D_FF, D_MODEL), D_FF ** -0.5),
        "b_down": nrm(ks[21], (DEPTH, N_EXPERTS, D_MODEL), 0.01),
        "g_final": gain(ks[22], (D_MODEL,)),
    }


def reference(x_prompt, x_sample, g_mix, w_in, rpb, conv_w, conv_b, w_rg_a, b_rg_a, w_rg_x, b_rg_x,
              rg_lambda, g_attn_out, g_lru_out, w_out, g_ffn, w_router, b_router, w_gate_up,
              b_gate_up, w_down, b_down, g_final):
    params = (g_mix, w_in, rpb, conv_w, conv_b, w_rg_a, b_rg_a, w_rg_x, b_rg_x, rg_lambda,
              g_attn_out, g_lru_out, w_out, g_ffn, w_router, b_router, w_gate_up, b_gate_up,
              w_down, b_down, g_final)
    y_prompt = trunk(x_prompt, *params)
    y_sample = trunk(x_sample, *params)
    return (y_prompt, y_sample)
```

```python
import functools

import jax
import jax.numpy as jnp
from jax import lax
from jax.experimental import pallas as pl
from jax.experimental.pallas import tpu as pltpu

F32 = jnp.float32
BF16 = jnp.bfloat16

GRID_W = 64
HEAD_DIM = 64
NA_ROWS = 8
NA_COLS = 16
CONV_W = 4
RG_C = 8.0
LRU_BLOCK_W = 128
TOP_K = 4
SWIGLU_LIMIT = 7.0
SWIGLU_ALPHA = 1.702
NORM_EPS = 1e-5
MASK_BIAS = -1e30

LANES = 128
SUBLANES = 8
BF16_SUBLANES = 16
VMEM_LIMIT = 56 * 1024 * 1024

MOE_SUB = 256
MOE_SUBS_PER_BLOCK = 4
MOE_TM = MOE_SUB * MOE_SUBS_PER_BLOCK
MOE_TF = 512


def _cparams(*sem):
    return pltpu.CompilerParams(dimension_semantics=sem, vmem_limit_bytes=VMEM_LIMIT)


def _rms(x, g):
    return x * lax.rsqrt(jnp.mean(x * x, axis=-1, keepdims=True) + NORM_EPS) * g


def _inproj_kernel(x_ref, g_ref, w_ref, o_ref, xn_ref):
    @pl.when(pl.program_id(1) == 0)
    def _():
        xn_ref[...] = _rms(x_ref[...], g_ref[...]).astype(BF16)

    o_ref[...] = jnp.dot(xn_ref[...], w_ref[...], preferred_element_type=F32).astype(o_ref.dtype)


def _inproj(x, g, w, *, tm, tn):
    n, d = x.shape
    wn = w.shape[1]
    return pl.pallas_call(
        _inproj_kernel,
        grid=(n // tm, wn // tn),
        in_specs=[pl.BlockSpec((tm, d), lambda i, j: (i, 0)),
                  pl.BlockSpec((1, d), lambda i, j: (0, 0)),
                  pl.BlockSpec((d, tn), lambda i, j: (0, j))],
        out_specs=pl.BlockSpec((tm, tn), lambda i, j: (i, j)),
        out_shape=jax.ShapeDtypeStruct((n, wn), BF16),
        scratch_shapes=[pltpu.VMEM((tm, d), BF16)],
        compiler_params=_cparams("arbitrary", "arbitrary"),
        name="inproj",
    )(x, g.reshape(1, d), w)


def _na_bias_table(rpb):
    h = rpb.shape[0]
    c = jnp.arange(GRID_W)
    col_start = jnp.clip(c - NA_COLS // 2, 0, GRID_W - NA_COLS)
    j = jnp.arange(GRID_W)
    inwin = (j[None, :] >= col_start[:, None]) & (j[None, :] < col_start[:, None] + NA_COLS)
    dc = jnp.clip(j[None, :] - c[:, None] + (NA_COLS - 1), 0, 2 * NA_COLS - 2)
    delta = jnp.arange(NA_ROWS)
    kk = jnp.arange(NA_ROWS)
    dr = kk[None, :] - delta[:, None] + (NA_ROWS - 1)
    t = rpb.astype(F32)[:, dr[:, :, None, None], dc[None, None, :, :]]
    t = jnp.where(inwin[None, None, None], t, MASK_BIAS)
    t = t.transpose(0, 1, 3, 2, 4).reshape(h // 2, 2, NA_ROWS, GRID_W, NA_ROWS * GRID_W)
    return t.transpose(0, 2, 1, 3, 4).reshape(h // 2, NA_ROWS, 2 * GRID_W, NA_ROWS * GRID_W)


def _na_kernel(q_ref, k_ref, v_ref, b_ref, o_ref, *, rows, rows_per_tile):
    ti = pl.program_id(2)
    lane = lax.broadcasted_iota(jnp.int32, (GRID_W, 2 * HEAD_DIM), 1)
    first_head = lane < HEAD_DIM
    win = NA_ROWS * GRID_W

    def body(rr, carry):
        r = ti * rows_per_tile + rr
        r0 = jnp.clip(r - NA_ROWS // 2, 0, rows - NA_ROWS)
        q = q_ref[pl.ds(pl.multiple_of(rr * GRID_W, GRID_W), GRID_W), :] * (HEAD_DIM ** -0.5)
        zero = jnp.zeros_like(q)
        qm = jnp.concatenate([jnp.where(first_head, q, zero), jnp.where(first_head, zero, q)], axis=0)
        ks = pl.multiple_of(r0 * GRID_W, GRID_W)
        kw = k_ref[pl.ds(ks, win), :]
        vw = v_ref[pl.ds(ks, win), :]
        s = lax.dot_general(qm, kw, (((1,), (1,)), ((), ())), preferred_element_type=F32)
        s = s + b_ref[0, r - r0]
        m = jnp.max(s, axis=-1, keepdims=True)
        p = jnp.exp(s - m)
        l = jnp.sum(p, axis=-1, keepdims=True)
        o = jnp.dot(p.astype(BF16), vw, preferred_element_type=F32) / l
        out = jnp.where(first_head, o[:GRID_W], o[GRID_W:])
        o_ref[pl.ds(pl.multiple_of(rr * GRID_W, GRID_W), GRID_W), :] = out.astype(o_ref.dtype)
        return carry

    lax.fori_loop(0, rows_per_tile, body, 0)


def _na(z, bias, *, row_off, batch, t, attn_w, tq):
    rows = t // GRID_W
    assert rows >= NA_ROWS and t % tq == 0 and row_off % t == 0 and tq % GRID_W == 0
    hp = attn_w // LANES
    nq = t // tq
    qoff, koff = row_off // tq, row_off // t
    kern = functools.partial(_na_kernel, rows=rows, rows_per_tile=tq // GRID_W)
    return pl.pallas_call(
        kern,
        grid=(batch, hp, nq),
        in_specs=[pl.BlockSpec((tq, LANES), lambda b, h, i: (qoff + b * nq + i, h)),
                  pl.BlockSpec((t, LANES), lambda b, h, i: (koff + b, hp + h)),
                  pl.BlockSpec((t, LANES), lambda b, h, i: (koff + b, 2 * hp + h)),
                  pl.BlockSpec((1,) + bias.shape[1:], lambda b, h, i: (h, 0, 0, 0))],
        out_specs=pl.BlockSpec((tq, LANES), lambda b, h, i: (b * nq + i, h)),
        out_shape=jax.ShapeDtypeStruct((batch * t, attn_w), BF16),
        compiler_params=_cparams("arbitrary", "arbitrary", "arbitrary"),
        name="na",
    )(z, z, z, bias)


def _softplus(x):
    return jnp.maximum(x, 0.0) + jnp.log(1.0 + jnp.exp(-jnp.abs(x)))


def _lru_kernel(u_ref, up_ref, un_ref, cw_ref, cb_ref, wa_ref, ba_ref, wx_ref, bx_ref, lam_ref, o_ref,
                ext, a_scr, b_scr, carry, *, tt, nt, t):
    d = pl.program_id(1)
    i = pl.program_id(2)
    ci = jnp.where(d == 0, i, nt - 1 - i)
    nblk = a_scr.shape[0]
    seg = tt // SUBLANES

    halo = SUBLANES
    prev = up_ref[...].astype(F32)[BF16_SUBLANES - halo:, :]
    nxt = un_ref[...].astype(F32)[:halo, :]
    ext[pl.ds(0, halo), :] = jnp.where(ci == 0, 0.0, prev)
    ext[pl.ds(halo, tt), :] = u_ref[...].astype(F32)
    ext[pl.ds(halo + tt, halo), :] = jnp.where(ci == nt - 1, 0.0, nxt)
    c = cb_ref[...]
    for j in range(CONV_W):
        c = c + cw_ref[pl.ds(j, 1), :] * ext[pl.ds(halo - CONV_W // 2 + j, tt), :]
    cbf = c.astype(BF16)

    tpos = ci * tt + lax.broadcasted_iota(jnp.int32, (tt, 1), 0)
    is_first = tpos == jnp.where(d == 0, 0, t - 1)
    sp = _softplus(-lam_ref[0])
    for n in range(nblk):
        sl = slice(n * LRU_BLOCK_W, (n + 1) * LRU_BLOCK_W)
        cn = cbf[:, sl]
        r = jax.nn.sigmoid(jnp.dot(cn, wa_ref[0, n], preferred_element_type=F32) + ba_ref[0, :, sl])
        ig = jax.nn.sigmoid(jnp.dot(cn, wx_ref[0, n], preferred_element_type=F32) + bx_ref[0, :, sl])
        a = jnp.exp(-RG_C * r * sp[:, sl])
        mult = jnp.where(is_first, 1.0, jnp.sqrt(1.0 - a * a))
        a_scr[n] = a
        b_scr[n] = mult * (ig * c[:, sl])

    @pl.when(i == 0)
    def _():
        carry[...] = jnp.zeros_like(carry)

    def step_index(jj):
        return jnp.where(d == 0, jj, seg - 1 - jj)

    def load(ref, n, j):
        return ref[n, pl.ds(j, SUBLANES, stride=seg), :]

    def local_step(jj, hp):
        j = step_index(jj)
        hs, ps = hp
        new_h, new_p = [], []
        for n in range(nblk):
            a = load(a_scr, n, j)
            new_h.append(a * hs[n] + load(b_scr, n, j))
            new_p.append(a * ps[n])
        return tuple(new_h), tuple(new_p)

    zeros = tuple(jnp.zeros((SUBLANES, LRU_BLOCK_W), F32) for _ in range(nblk))
    ones = tuple(jnp.ones((SUBLANES, LRU_BLOCK_W), F32) for _ in range(nblk))
    h_end, p_end = lax.fori_loop(0, seg, local_step, (zeros, ones))

    cin, cout = [], []
    for n in range(nblk):
        c0 = carry[pl.ds(n, 1), :]
        fwd, cur = [], c0
        for s in range(SUBLANES):
            fwd.append(cur)
            cur = h_end[n][s:s + 1] + p_end[n][s:s + 1] * cur
        fwd_out = cur
        bwd, cur = [None] * SUBLANES, c0
        for s in reversed(range(SUBLANES)):
            bwd[s] = cur
            cur = h_end[n][s:s + 1] + p_end[n][s:s + 1] * cur
        bwd_out = cur
        cin.append(jnp.where(d == 0, jnp.concatenate(fwd, axis=0), jnp.concatenate(bwd, axis=0)))
        cout.append(jnp.where(d == 0, fwd_out, bwd_out))

    def final_step(jj, hs):
        j = step_index(jj)
        new_h = []
        for n in range(nblk):
            h = load(a_scr, n, j) * hs[n] + load(b_scr, n, j)
            o_ref[0, n, pl.ds(j, SUBLANES, stride=seg), :] = h
            new_h.append(h)
        return tuple(new_h)

    lax.fori_loop(0, seg, final_step, tuple(cin))
    for n in range(nblk):
        carry[pl.ds(n, 1), :] = cout[n]


def _lru(z, cw, cb, wa, ba, wx, bx, lam, *, row_off, batch, t, col_blk, tt):
    c = cw.shape[1]
    nblk = c // LRU_BLOCK_W
    nt = t // tt
    ntot = z.shape[0]
    assert t % tt == 0 and row_off % tt == 0 and tt % (SUBLANES * SUBLANES) == 0
    hb = BF16_SUBLANES
    base = row_off // tt

    def chunk(dd, i):
        return jnp.where(dd == 0, i, nt - 1 - i)

    def u_map(b, dd, i):
        return (base + b * nt + chunk(dd, i), col_blk)

    def prev_map(b, dd, i):
        return (jnp.maximum((row_off + b * t + chunk(dd, i) * tt) // hb - 1, 0), col_blk)

    def next_map(b, dd, i):
        return (jnp.minimum((row_off + b * t + (chunk(dd, i) + 1) * tt) // hb, ntot // hb - 1), col_blk)

    kern = functools.partial(_lru_kernel, tt=tt, nt=nt, t=t)
    dir_w = lambda b, dd, i: (dd, 0, 0, 0)
    dir_v = lambda b, dd, i: (dd, 0, 0)
    return pl.pallas_call(
        kern,
        grid=(batch, 2, nt),
        in_specs=[pl.BlockSpec((tt, c), u_map),
                  pl.BlockSpec((hb, c), prev_map),
                  pl.BlockSpec((hb, c), next_map),
                  pl.BlockSpec((CONV_W, c), lambda b, dd, i: (0, 0)),
                  pl.BlockSpec((1, c), lambda b, dd, i: (0, 0)),
                  pl.BlockSpec((1, nblk, LRU_BLOCK_W, LRU_BLOCK_W), dir_w),
                  pl.BlockSpec((1, 1, c), dir_v),
                  pl.BlockSpec((1, nblk, LRU_BLOCK_W, LRU_BLOCK_W), dir_w),
                  pl.BlockSpec((1, 1, c), dir_v),
                  pl.BlockSpec((1, 1, c), dir_v)],
        out_specs=pl.BlockSpec((1, nblk, tt, LRU_BLOCK_W), lambda b, dd, i: (dd, 0, b * nt + chunk(dd, i), 0)),
        out_shape=jax.ShapeDtypeStruct((2, nblk, batch * t, LRU_BLOCK_W), F32),
        scratch_shapes=[pltpu.VMEM((tt + 2 * SUBLANES, c), F32),
                        pltpu.VMEM((nblk, tt, LRU_BLOCK_W), F32),
                        pltpu.VMEM((nblk, tt, LRU_BLOCK_W), F32),
                        pltpu.VMEM((nblk, LRU_BLOCK_W), F32)],
        compiler_params=_cparams("arbitrary", "arbitrary", "arbitrary"),
        name="lru",
    )(z, z, z, cw, cb.reshape(1, c), wa, ba.reshape(2, 1, c), wx, bx.reshape(2, 1, c), lam.reshape(2, 1, c))


def _gelu_tanh(x):
    return 0.5 * x * (1.0 + jnp.tanh(0.7978845608028654 * (x + 0.044715 * (x * x * x))))


def _outproj_kernel(attn_ref, hf_ref, hb_ref, g_ref, x_ref, ga_ref, gl_ref, wo_ref, gf_ref, wr_ref, br_ref,
                    x1_ref, h2_ref, idx_ref, gate_ref):
    nblk = hf_ref.shape[1]
    an = _rms(attn_ref[...].astype(F32), ga_ref[...])
    h = jnp.concatenate([hf_ref[0, n] + hb_ref[0, n] for n in range(nblk)], axis=-1)
    rn = _rms(h * _gelu_tanh(g_ref[...].astype(F32)), gl_ref[...])
    mixed = jnp.concatenate([an, rn], axis=-1).astype(BF16)
    x1 = x_ref[...] + jnp.dot(mixed, wo_ref[...], preferred_element_type=F32)
    x1_ref[...] = x1
    h2 = _rms(x1, gf_ref[...])
    h2_ref[...] = h2

    logits = jnp.dot(h2, wr_ref[...], preferred_element_type=F32, precision=lax.Precision.HIGHEST) + br_ref[...]
    ne = logits.shape[-1]
    eidx = lax.broadcasted_iota(jnp.int32, logits.shape, 1)
    kidx = lax.broadcasted_iota(jnp.int32, idx_ref.shape, 1)
    vals = jnp.zeros(gate_ref.shape, F32)
    idxs = jnp.zeros(idx_ref.shape, jnp.int32)
    cur = logits
    for k in range(TOP_K):
        m = jnp.max(cur, axis=-1, keepdims=True)
        sel = jnp.min(jnp.where(cur == m, eidx, ne), axis=-1, keepdims=True)
        vals = jnp.where(kidx == k, m, vals)
        idxs = jnp.where(kidx == k, sel, idxs)
        cur = jnp.where(eidx == sel, -jnp.inf, cur)
    e = jnp.exp(vals - vals[:, 0:1])
    gate_ref[...] = e / jnp.sum(e, axis=-1, keepdims=True)
    idx_ref[...] = idxs


def _outproj(attn, h, z, x, ga, gl, wo, gf, wr, br, *, g_blk, tm):
    n, d = x.shape
    aw = attn.shape[1]
    nblk = h.shape[1]
    c = nblk * LRU_BLOCK_W
    ne = wr.shape[1]
    const = lambda i: (0, 0)
    return pl.pallas_call(
        _outproj_kernel,
        grid=(n // tm,),
        in_specs=[pl.BlockSpec((tm, aw), lambda i: (i, 0)),
                  pl.BlockSpec((1, nblk, tm, LRU_BLOCK_W), lambda i: (0, 0, i, 0)),
                  pl.BlockSpec((1, nblk, tm, LRU_BLOCK_W), lambda i: (1, 0, i, 0)),
                  pl.BlockSpec((tm, c), lambda i: (i, g_blk)),
                  pl.BlockSpec((tm, d), lambda i: (i, 0)),
                  pl.BlockSpec((1, aw), const),
                  pl.BlockSpec((1, c), const),
                  pl.BlockSpec((d, d), const),
                  pl.BlockSpec((1, d), const),
                  pl.BlockSpec((d, ne), const),
                  pl.BlockSpec((1, ne), const)],
        out_specs=[pl.BlockSpec((tm, d), lambda i: (i, 0)),
                   pl.BlockSpec((tm, d), lambda i: (i, 0)),
                   pl.BlockSpec((tm, TOP_K), lambda i: (i, 0)),
                   pl.BlockSpec((tm, TOP_K), lambda i: (i, 0))],
        out_shape=[jax.ShapeDtypeStruct((n, d), F32),
                   jax.ShapeDtypeStruct((n, d), F32),
                   jax.ShapeDtypeStruct((n, TOP_K), jnp.int32),
                   jax.ShapeDtypeStruct((n, TOP_K), F32)],
        compiler_params=_cparams("arbitrary"),
        name="outproj",
    )(attn, h, h, z, x, ga.reshape(1, aw), gl.reshape(1, c), wo, gf.reshape(1, d), wr, br.reshape(1, ne))


def _moe_plan(top_idx, n_experts):
    n = top_idx.shape[0]
    m = n * TOP_K
    n_sb = -(-m // MOE_TM) + n_experts
    flat_e = top_idx.reshape(-1)
    order = jnp.argsort(flat_e).astype(jnp.int32)
    counts = jnp.sum((flat_e[:, None] == jnp.arange(n_experts)[None, :]).astype(jnp.int32), axis=0)
    start = jnp.cumsum(counts) - counts
    sb_per_e = (counts + MOE_TM - 1) // MOE_TM
    sb_end = jnp.cumsum(sb_per_e)
    sb = jnp.arange(n_sb, dtype=jnp.int32)
    sb_e = jnp.minimum(jnp.searchsorted(sb_end, sb, side="right"), n_experts - 1).astype(jnp.int32)
    local = sb - (sb_end - sb_per_e)[sb_e]
    valid = jnp.clip(counts[sb_e] - local * MOE_TM, 0, MOE_TM)
    valid = jnp.where(sb < sb_end[-1], valid, 0).astype(jnp.int32)
    r = jnp.arange(MOE_TM, dtype=jnp.int32)
    live = r[None, :] < valid[:, None]
    sorted_pos = jnp.clip(start[sb_e][:, None] + local[:, None] * MOE_TM + r[None, :], 0, m - 1)
    a = order[sorted_pos]
    src = jnp.where(live, a // TOP_K, 0)
    dst = jnp.where(live, (a % TOP_K) * n + a // TOP_K, 0)
    rows = jnp.concatenate([src, dst], axis=1).reshape(-1).astype(jnp.int32)
    return sb_e, valid, rows


def _moe_kernel(sbe_ref, live_ref, rows_hbm, h_hbm, wg_ref, wu_ref, bg_ref, bu_ref, wd_ref, bd_ref, y_hbm,
                rows_smem, xg, xb, acc, sem_idx, sem_in, sem_out):
    s = pl.program_id(0)
    j = pl.program_id(1)
    nj = pl.num_programs(1)
    nlive = live_ref[s]
    nsub = (nlive + MOE_SUB - 1) // MOE_SUB

    def sub_rows(q):
        return pl.ds(pl.multiple_of(q * MOE_SUB, MOE_SUB), MOE_SUB)

    def wait_rows(src, dst, sem):
        def block(q, c):
            pltpu.make_async_copy(src.at[pl.ds(0, MOE_SUB), :], dst.at[pl.ds(0, MOE_SUB), :], sem).wait()
            return c

        def row(q, c):
            pltpu.make_async_copy(src.at[pl.ds(0, 1), :], dst.at[pl.ds(0, 1), :], sem).wait()
            return c

        nfull = nlive // MOE_SUB
        lax.fori_loop(0, nfull, block, 0)
        lax.fori_loop(0, nlive - nfull * MOE_SUB, row, 0)

    @pl.when(jnp.logical_and(s == 0, j == 0))
    def _():
        xg[...] = jnp.zeros_like(xg)

    @pl.when(jnp.logical_and(nsub > 0, j == 0))
    def _gather():
        cp = pltpu.make_async_copy(rows_hbm.at[pl.ds(pl.multiple_of(s * 2 * MOE_TM, 2 * MOE_TM), 2 * MOE_TM)],
                                   rows_smem, sem_idx)
        cp.start()
        cp.wait()

        def issue(row, c):
            tok = rows_smem[row]
            pltpu.make_async_copy(h_hbm.at[pl.ds(tok, 1), :], xg.at[pl.ds(row, 1), :], sem_in).start()
            return c

        lax.fori_loop(0, nlive, issue, 0)
        wait_rows(h_hbm, xg, sem_in)

        def cast(q, c):
            xb[sub_rows(q), :] = xg[sub_rows(q), :].astype(BF16)
            return c

        lax.fori_loop(0, nsub, cast, 0)

    def compute(q, c):
        x = xb[sub_rows(q), :]
        gate = jnp.dot(x, wg_ref[0], preferred_element_type=F32) + bg_ref[0]
        up = jnp.dot(x, wu_ref[0], preferred_element_type=F32) + bu_ref[0]
        gate = jnp.minimum(gate, SWIGLU_LIMIT)
        up = jnp.clip(up, -SWIGLU_LIMIT, SWIGLU_LIMIT)
        act = (up + 1.0) * (gate * jax.nn.sigmoid(SWIGLU_ALPHA * gate))
        y = jnp.dot(act.astype(BF16), wd_ref[0], preferred_element_type=F32)

        @pl.when(j == 0)
        def _():
            acc[sub_rows(q), :] = y + bd_ref[0]

        @pl.when(j > 0)
        def _():
            acc[sub_rows(q), :] += y

        return c

    lax.fori_loop(0, nsub, compute, 0)

    @pl.when(jnp.logical_and(nsub > 0, j == nj - 1))
    def _scatter():
        def issue(row, c):
            dst = rows_smem[MOE_TM + row]
            pltpu.make_async_copy(acc.at[pl.ds(row, 1), :], y_hbm.at[pl.ds(dst, 1), :], sem_out).start()
            return c

        lax.fori_loop(0, nlive, issue, 0)
        wait_rows(acc, y_hbm, sem_out)


def _moe(h2, sb_e, live, rows, wgu, bgu, wdn, bdn):
    n, d = h2.shape
    ne, f = wdn.shape[0], wdn.shape[1]
    nj = f // MOE_TF
    n_sb = sb_e.shape[0]

    def ff(s, j, live_ref):
        return jnp.where(live_ref[s] > 0, j, nj - 1)

    grid_spec = pltpu.PrefetchScalarGridSpec(
        num_scalar_prefetch=2,
        grid=(n_sb, nj),
        in_specs=[pl.BlockSpec(memory_space=pl.ANY),
                  pl.BlockSpec(memory_space=pl.ANY),
                  pl.BlockSpec((1, d, MOE_TF), lambda s, j, e, ns: (e[s], 0, ff(s, j, ns))),
                  pl.BlockSpec((1, d, MOE_TF), lambda s, j, e, ns: (e[s], 0, nj + ff(s, j, ns))),
                  pl.BlockSpec((1, 1, MOE_TF), lambda s, j, e, ns: (e[s], 0, ff(s, j, ns))),
                  pl.BlockSpec((1, 1, MOE_TF), lambda s, j, e, ns: (e[s], 0, nj + ff(s, j, ns))),
                  pl.BlockSpec((1, MOE_TF, d), lambda s, j, e, ns: (e[s], ff(s, j, ns), 0)),
                  pl.BlockSpec((1, 1, d), lambda s, j, e, ns: (e[s], 0, 0))],
        out_specs=pl.BlockSpec(memory_space=pl.ANY),
        scratch_shapes=[pltpu.SMEM((2 * MOE_TM,), jnp.int32),
                        pltpu.VMEM((MOE_TM, d), F32),
                        pltpu.VMEM((MOE_TM, d), BF16),
                        pltpu.VMEM((MOE_TM, d), F32),
                        pltpu.SemaphoreType.DMA(()),
                        pltpu.SemaphoreType.DMA(()),
                        pltpu.SemaphoreType.DMA(())])
    return pl.pallas_call(
        _moe_kernel,
        grid_spec=grid_spec,
        out_shape=jax.ShapeDtypeStruct((TOP_K * n, d), F32),
        compiler_params=_cparams("arbitrary", "arbitrary"),
        name="moe",
    )(sb_e, live, rows, h2, wgu, wgu, bgu, bgu, wdn, bdn)


def _combine_kernel(x_ref, y0_ref, y1_ref, y2_ref, y3_ref, gate_ref, gfin_ref, o_ref, *, final):
    g = gate_ref[...]
    x = x_ref[...]
    for k, y_ref in enumerate((y0_ref, y1_ref, y2_ref, y3_ref)):
        x = x + g[:, k:k + 1] * y_ref[...]
    o_ref[...] = _rms(x, gfin_ref[...]) if final else x


def _combine(x1, y, gates, gfin, *, final, tm):
    n, d = x1.shape
    nb = n // tm
    y_spec = lambda k: pl.BlockSpec((tm, d), lambda i: (k * nb + i, 0))
    return pl.pallas_call(
        functools.partial(_combine_kernel, final=final),
        grid=(nb,),
        in_specs=[pl.BlockSpec((tm, d), lambda i: (i, 0)), y_spec(0), y_spec(1), y_spec(2), y_spec(3),
                  pl.BlockSpec((tm, TOP_K), lambda i: (i, 0)),
                  pl.BlockSpec((1, d), lambda i: (0, 0))],
        out_specs=pl.BlockSpec((tm, d), lambda i: (i, 0)),
        out_shape=jax.ShapeDtypeStruct((n, d), F32),
        compiler_params=_cparams("arbitrary"),
        name="combine",
    )(x1, y, y, y, y, gates, gfin.reshape(1, d))


def _pick(n, *cands):
    for c in cands:
        if n % c == 0:
            return c
    raise ValueError(f"no tile for {n}")


def kernel(x_prompt, x_sample, g_mix, w_in, rpb, conv_w, conv_b, w_rg_a, b_rg_a, w_rg_x, b_rg_x, rg_lambda,
           g_attn_out, g_lru_out, w_out, g_ffn, w_router, b_router, w_gate_up, b_gate_up, w_down, b_down, g_final):
    depth = w_in.shape[0]
    d = x_prompt.shape[-1]
    attn_w = rpb.shape[1] * HEAD_DIM
    lru_w = conv_w.shape[-1]
    ne = w_router.shape[-1]
    f = w_down.shape[2]
    seqs = []
    off = 0
    for xs in (x_prompt, x_sample):
        seqs.append((off, xs.shape[0], xs.shape[1]))
        off += xs.shape[0] * xs.shape[1]
    n = off
    x = jnp.concatenate([x_prompt.reshape(-1, d), x_sample.reshape(-1, d)], axis=0)
    assert attn_w % LANES == 0 and lru_w == attn_w, "column blocks of z assume equal attention / recurrent widths"

    for l in range(depth):
        z = _inproj(x, g_mix[l], w_in[l].astype(BF16), tm=_pick(n, 1024, 512, 256), tn=_pick(w_in.shape[2], 1024, 512, 256))
        bias = _na_bias_table(rpb[l])
        wa, wx = w_rg_a[l].astype(BF16), w_rg_x[l].astype(BF16)
        attn, h = [], []
        for (o, b, t) in seqs:
            attn.append(_na(z, bias, row_off=o, batch=b, t=t, attn_w=attn_w, tq=_pick(t, 512, 256, 128, 64)))
            h.append(_lru(z, conv_w[l], conv_b[l], wa, b_rg_a[l], wx, b_rg_x[l], rg_lambda[l],
                          row_off=o, batch=b, t=t, col_blk=3, tt=_pick(t, 512, 256, 128, 64)))
        attn = jnp.concatenate(attn, axis=0)
        h = jnp.concatenate(h, axis=2)
        x1, h2, top_idx, gates = _outproj(attn, h, z, x, g_attn_out[l], g_lru_out[l], w_out[l].astype(BF16),
                                          g_ffn[l], w_router[l], b_router[l], g_blk=4, tm=_pick(n, 256, 128))
        sb_e, live, rows = _moe_plan(top_idx, ne)
        wgu = jnp.concatenate([w_gate_up[l][..., 0::2], w_gate_up[l][..., 1::2]], axis=-1).astype(BF16)
        bgu = jnp.concatenate([b_gate_up[l][..., 0::2], b_gate_up[l][..., 1::2]], axis=-1).reshape(ne, 1, 2 * f)
        y = _moe(h2, sb_e, live, rows, wgu, bgu, w_down[l].astype(BF16), b_down[l].reshape(ne, 1, d))
        x = _combine(x1, y, gates, g_final, final=(l == depth - 1), tm=_pick(n, 256, 128))

    n0 = x_prompt.shape[0] * x_prompt.shape[1]
    return x[:n0].reshape(x_prompt.shape), x[n0:].reshape(x_sample.shape)
```

```python
import functools

import jax
import jax.numpy as jnp
from jax import lax
from jax.experimental import pallas as pl
from jax.experimental.pallas import tpu as pltpu

F32 = jnp.float32
BF16 = jnp.bfloat16

GRID_W = 64
HEAD_DIM = 64
NA_ROWS = 8
NA_COLS = 16
CONV_W = 4
RG_C = 8.0
LRU_BLOCK_W = 128
TOP_K = 4
SWIGLU_LIMIT = 7.0
SWIGLU_ALPHA = 1.702
NORM_EPS = 1e-5
MASK_BIAS = -1e30

LANES = 128
SUBLANES = 8
BF16_SUBLANES = 16
VMEM_LIMIT = 56 * 1024 * 1024

MOE_SUB = 256
MOE_SUBS_PER_BLOCK = 4
MOE_TM = MOE_SUB * MOE_SUBS_PER_BLOCK
MOE_TF = 512
MOE_ISSUE_UNROLL = 4


def _cparams(*sem):
    return pltpu.CompilerParams(dimension_semantics=sem, vmem_limit_bytes=VMEM_LIMIT)


def _rms(x, g):
    return x * lax.rsqrt(jnp.mean(x * x, axis=-1, keepdims=True) + NORM_EPS) * g


def _inproj_kernel(x_ref, g_ref, w_ref, o_ref, xn_ref):
    @pl.when(pl.program_id(1) == 0)
    def _():
        xn_ref[...] = _rms(x_ref[...], g_ref[...]).astype(BF16)

    o_ref[...] = jnp.dot(xn_ref[...], w_ref[...], preferred_element_type=F32).astype(o_ref.dtype)


def _inproj(x, g, w, *, tm, tn):
    n, d = x.shape
    wn = w.shape[1]
    return pl.pallas_call(
        _inproj_kernel,
        grid=(n // tm, wn // tn),
        in_specs=[pl.BlockSpec((tm, d), lambda i, j: (i, 0)),
                  pl.BlockSpec((1, d), lambda i, j: (0, 0)),
                  pl.BlockSpec((d, tn), lambda i, j: (0, j))],
        out_specs=pl.BlockSpec((tm, tn), lambda i, j: (i, j)),
        out_shape=jax.ShapeDtypeStruct((n, wn), BF16),
        scratch_shapes=[pltpu.VMEM((tm, d), BF16)],
        compiler_params=_cparams("arbitrary", "arbitrary"),
        name="inproj",
    )(x, g.reshape(1, d), w)


def _na_bias_table(rpb):
    h = rpb.shape[0]
    c = jnp.arange(GRID_W)
    col_start = jnp.clip(c - NA_COLS // 2, 0, GRID_W - NA_COLS)
    j = jnp.arange(GRID_W)
    inwin = (j[None, :] >= col_start[:, None]) & (j[None, :] < col_start[:, None] + NA_COLS)
    dc = jnp.clip(j[None, :] - c[:, None] + (NA_COLS - 1), 0, 2 * NA_COLS - 2)
    delta = jnp.arange(NA_ROWS)
    kk = jnp.arange(NA_ROWS)
    dr = kk[None, :] - delta[:, None] + (NA_ROWS - 1)
    row_sel = jax.nn.one_hot(dr, 2 * NA_ROWS - 1, dtype=F32)
    col_sel = jax.nn.one_hot(dc, 2 * NA_COLS - 1, dtype=F32)
    t = jnp.einsum("dka,hab,cjb->hdkcj", row_sel, rpb.astype(F32), col_sel, precision=lax.Precision.HIGHEST)
    t = jnp.where(inwin[None, None, None], t, MASK_BIAS)
    t = t.transpose(0, 1, 3, 2, 4).reshape(h // 2, 2, NA_ROWS, GRID_W, NA_ROWS * GRID_W)
    return t.transpose(0, 2, 1, 3, 4).reshape(h // 2, NA_ROWS, 2 * GRID_W, NA_ROWS * GRID_W)


def _na_kernel(q_ref, k_ref, v_ref, b_ref, o_ref, *, rows, rows_per_tile):
    ti = pl.program_id(2)
    lane = lax.broadcasted_iota(jnp.int32, (GRID_W, 2 * HEAD_DIM), 1)
    first_head = lane < HEAD_DIM
    win = NA_ROWS * GRID_W

    def body(rr, carry):
        r = ti * rows_per_tile + rr
        r0 = jnp.clip(r - NA_ROWS // 2, 0, rows - NA_ROWS)
        q = q_ref[pl.ds(pl.multiple_of(rr * GRID_W, GRID_W), GRID_W), :] * (HEAD_DIM ** -0.5)
        zero = jnp.zeros_like(q)
        qm = jnp.concatenate([jnp.where(first_head, q, zero), jnp.where(first_head, zero, q)], axis=0)
        ks = pl.multiple_of(r0 * GRID_W, GRID_W)
        kw = k_ref[pl.ds(ks, win), :]
        vw = v_ref[pl.ds(ks, win), :]
        s = lax.dot_general(qm, kw, (((1,), (1,)), ((), ())), preferred_element_type=F32)
        s = s + b_ref[0, r - r0]
        m = jnp.max(s, axis=-1, keepdims=True)
        p = jnp.exp(s - m)
        l = jnp.sum(p, axis=-1, keepdims=True)
        o = jnp.dot(p.astype(BF16), vw, preferred_element_type=F32) / l
        out = jnp.where(first_head, o[:GRID_W], o[GRID_W:])
        o_ref[pl.ds(pl.multiple_of(rr * GRID_W, GRID_W), GRID_W), :] = out.astype(o_ref.dtype)
        return carry

    lax.fori_loop(0, rows_per_tile, body, 0, unroll=True)


def _na(z, bias, *, row_off, batch, t, attn_w, tq):
    rows = t // GRID_W
    assert rows >= NA_ROWS and t % tq == 0 and row_off % t == 0 and tq % GRID_W == 0
    hp = attn_w // LANES
    nq = t // tq
    qoff, koff = row_off // tq, row_off // t
    kern = functools.partial(_na_kernel, rows=rows, rows_per_tile=tq // GRID_W)
    return pl.pallas_call(
        kern,
        grid=(batch, hp, nq),
        in_specs=[pl.BlockSpec((tq, LANES), lambda b, h, i: (qoff + b * nq + i, h)),
                  pl.BlockSpec((t, LANES), lambda b, h, i: (koff + b, hp + h)),
                  pl.BlockSpec((t, LANES), lambda b, h, i: (koff + b, 2 * hp + h)),
                  pl.BlockSpec((1,) + bias.shape[1:], lambda b, h, i: (h, 0, 0, 0))],
        out_specs=pl.BlockSpec((tq, LANES), lambda b, h, i: (b * nq + i, h)),
        out_shape=jax.ShapeDtypeStruct((batch * t, attn_w), BF16),
        compiler_params=_cparams("arbitrary", "arbitrary", "arbitrary"),
        name="na",
    )(z, z, z, bias)


def _softplus(x):
    return jnp.maximum(x, 0.0) + jnp.log(1.0 + jnp.exp(-jnp.abs(x)))


def _lru_coeffs(u, prev, nxt, ext, refs, *, d, ci, tt, nt, t, store):
    cw_ref, cb_ref, wa_ref, ba_ref, wx_ref, bx_ref, lam_ref = refs
    halo = SUBLANES
    ext[pl.ds(0, halo), :] = jnp.where(ci == 0, 0.0, prev)
    ext[pl.ds(halo, tt), :] = u
    ext[pl.ds(halo + tt, halo), :] = jnp.where(ci == nt - 1, 0.0, nxt)
    c = cb_ref[...]
    for j in range(CONV_W):
        c = c + cw_ref[pl.ds(j, 1), :] * ext[pl.ds(halo - CONV_W // 2 + j, tt), :]
    cbf = c.astype(BF16)

    tpos = ci * tt + lax.broadcasted_iota(jnp.int32, (tt, 1), 0)
    is_first = tpos == jnp.where(d == 0, 0, t - 1)
    sp = _softplus(-lam_ref[0])
    for n in range(wa_ref.shape[1]):
        sl = slice(n * LRU_BLOCK_W, (n + 1) * LRU_BLOCK_W)
        cn = cbf[:, sl]
        r = jax.nn.sigmoid(jnp.dot(cn, wa_ref[0, n], preferred_element_type=F32) + ba_ref[0, :, sl])
        ig = jax.nn.sigmoid(jnp.dot(cn, wx_ref[0, n], preferred_element_type=F32) + bx_ref[0, :, sl])
        a = jnp.exp(-RG_C * r * sp[:, sl])
        mult = jnp.where(is_first, 1.0, jnp.sqrt(1.0 - a * a))
        store(n, a, mult * (ig * c[:, sl]))


def _lru_batched_kernel(u_ref, up_ref, un_ref, cw_ref, cb_ref, wa_ref, ba_ref, wx_ref, bx_ref, lam_ref, o_ref,
                        ext, a_scr, b_scr, carry, *, tt, nt, t):
    d = pl.program_id(0)
    i = pl.program_id(1)
    ci = jnp.where(d == 0, i, nt - 1 - i)
    nblk = a_scr.shape[0]
    refs = (cw_ref, cb_ref, wa_ref, ba_ref, wx_ref, bx_ref, lam_ref)

    def per_sequence(b, c0):
        def store(n, a, bb):
            a_scr[n, :, b, :] = a
            b_scr[n, :, b, :] = bb

        _lru_coeffs(u_ref[b].astype(F32), up_ref[b].astype(F32)[BF16_SUBLANES - SUBLANES:, :],
                    un_ref[b].astype(F32)[:SUBLANES, :], ext, refs, d=d, ci=ci, tt=tt, nt=nt, t=t, store=store)
        return c0

    lax.fori_loop(0, SUBLANES, per_sequence, 0)

    @pl.when(i == 0)
    def _():
        carry[...] = jnp.zeros_like(carry)

    def step(jj, hs):
        j = jnp.where(d == 0, jj, tt - 1 - jj)
        new = []
        for n in range(nblk):
            h = a_scr[n, j] * hs[n] + b_scr[n, j]
            b_scr[n, j] = h
            new.append(h)
        return tuple(new)

    hs = lax.fori_loop(0, tt, step, tuple(carry[n] for n in range(nblk)))
    for n in range(nblk):
        carry[n] = hs[n]

    def write(b, c0):
        for n in range(nblk):
            o_ref[0, n, b] = b_scr[n, :, b, :]
        return c0

    lax.fori_loop(0, SUBLANES, write, 0)


def _lru_batched(z, cw, cb, wa, ba, wx, bx, lam, *, row_off, t, col_blk, tt):
    c = cw.shape[1]
    nblk = c // LRU_BLOCK_W
    nt = t // tt
    ntot = z.shape[0]
    nb = SUBLANES
    assert t % tt == 0 and ntot % t == 0 and row_off % (nb * t) == 0 and tt % BF16_SUBLANES == 0
    z3 = z.reshape(ntot // t, t, z.shape[1])
    g = row_off // (nb * t)
    hb = BF16_SUBLANES

    def chunk(dd, i):
        return jnp.where(dd == 0, i, nt - 1 - i)

    kern = functools.partial(_lru_batched_kernel, tt=tt, nt=nt, t=t)
    dir_w = lambda dd, i: (dd, 0, 0, 0)
    dir_v = lambda dd, i: (dd, 0, 0)
    out = pl.pallas_call(
        kern,
        grid=(2, nt),
        in_specs=[pl.BlockSpec((nb, tt, c), lambda dd, i: (g, chunk(dd, i), col_blk)),
                  pl.BlockSpec((nb, hb, c), lambda dd, i: (g, jnp.maximum(chunk(dd, i) * (tt // hb) - 1, 0), col_blk)),
                  pl.BlockSpec((nb, hb, c), lambda dd, i: (g, jnp.minimum((chunk(dd, i) + 1) * (tt // hb), t // hb - 1), col_blk)),
                  pl.BlockSpec((CONV_W, c), lambda dd, i: (0, 0)),
                  pl.BlockSpec((1, c), lambda dd, i: (0, 0)),
                  pl.BlockSpec((1, nblk, LRU_BLOCK_W, LRU_BLOCK_W), dir_w),
                  pl.BlockSpec((1, 1, c), dir_v),
                  pl.BlockSpec((1, nblk, LRU_BLOCK_W, LRU_BLOCK_W), dir_w),
                  pl.BlockSpec((1, 1, c), dir_v),
                  pl.BlockSpec((1, 1, c), dir_v)],
        out_specs=pl.BlockSpec((1, nblk, nb, tt, LRU_BLOCK_W), lambda dd, i: (dd, 0, 0, chunk(dd, i), 0)),
        out_shape=jax.ShapeDtypeStruct((2, nblk, nb, t, LRU_BLOCK_W), F32),
        scratch_shapes=[pltpu.VMEM((tt + 2 * SUBLANES, c), F32),
                        pltpu.VMEM((nblk, tt, nb, LRU_BLOCK_W), F32),
                        pltpu.VMEM((nblk, tt, nb, LRU_BLOCK_W), F32),
                        pltpu.VMEM((nblk, nb, LRU_BLOCK_W), F32)],
        compiler_params=_cparams("arbitrary", "arbitrary"),
        name="lru_batched",
    )(z3, z3, z3, cw, cb.reshape(1, c), wa, ba.reshape(2, 1, c), wx, bx.reshape(2, 1, c), lam.reshape(2, 1, c))
    return out.reshape(2, nblk, nb * t, LRU_BLOCK_W)


def _lru_kernel(u_ref, up_ref, un_ref, cw_ref, cb_ref, wa_ref, ba_ref, wx_ref, bx_ref, lam_ref, o_ref,
                ext, a_scr, b_scr, carry, *, tt, nt, t):
    d = pl.program_id(1)
    i = pl.program_id(2)
    ci = jnp.where(d == 0, i, nt - 1 - i)
    nblk = a_scr.shape[0]
    seg = tt // SUBLANES

    def store(n, a, bb):
        a_scr[n] = a
        b_scr[n] = bb

    _lru_coeffs(u_ref[...].astype(F32), up_ref[...].astype(F32)[BF16_SUBLANES - SUBLANES:, :],
                un_ref[...].astype(F32)[:SUBLANES, :], ext,
                (cw_ref, cb_ref, wa_ref, ba_ref, wx_ref, bx_ref, lam_ref), d=d, ci=ci, tt=tt, nt=nt, t=t, store=store)

    @pl.when(i == 0)
    def _():
        carry[...] = jnp.zeros_like(carry)

    def step_index(jj):
        return jnp.where(d == 0, jj, seg - 1 - jj)

    def load(ref, n, j):
        return ref[n, pl.ds(j, SUBLANES, stride=seg), :]

    def local_step(jj, hp):
        j = step_index(jj)
        hs, ps = hp
        new_h, new_p = [], []
        for n in range(nblk):
            a = load(a_scr, n, j)
            new_h.append(a * hs[n] + load(b_scr, n, j))
            new_p.append(a * ps[n])
        return tuple(new_h), tuple(new_p)

    zeros = tuple(jnp.zeros((SUBLANES, LRU_BLOCK_W), F32) for _ in range(nblk))
    ones = tuple(jnp.ones((SUBLANES, LRU_BLOCK_W), F32) for _ in range(nblk))
    h_end, p_end = lax.fori_loop(0, seg, local_step, (zeros, ones))

    cin, cout = [], []
    for n in range(nblk):
        c0 = carry[pl.ds(n, 1), :]
        fwd, cur = [], c0
        for s in range(SUBLANES):
            fwd.append(cur)
            cur = h_end[n][s:s + 1] + p_end[n][s:s + 1] * cur
        fwd_out = cur
        bwd, cur = [None] * SUBLANES, c0
        for s in reversed(range(SUBLANES)):
            bwd[s] = cur
            cur = h_end[n][s:s + 1] + p_end[n][s:s + 1] * cur
        bwd_out = cur
        cin.append(jnp.where(d == 0, jnp.concatenate(fwd, axis=0), jnp.concatenate(bwd, axis=0)))
        cout.append(jnp.where(d == 0, fwd_out, bwd_out))

    def final_step(jj, hs):
        j = step_index(jj)
        new_h = []
        for n in range(nblk):
            h = load(a_scr, n, j) * hs[n] + load(b_scr, n, j)
            o_ref[0, n, pl.ds(j, SUBLANES, stride=seg), :] = h
            new_h.append(h)
        return tuple(new_h)

    lax.fori_loop(0, seg, final_step, tuple(cin))
    for n in range(nblk):
        carry[pl.ds(n, 1), :] = cout[n]


def _lru(z, cw, cb, wa, ba, wx, bx, lam, *, row_off, batch, t, col_blk, tt):
    c = cw.shape[1]
    nblk = c // LRU_BLOCK_W
    nt = t // tt
    ntot = z.shape[0]
    assert t % tt == 0 and row_off % tt == 0 and tt % (SUBLANES * SUBLANES) == 0
    hb = BF16_SUBLANES
    base = row_off // tt

    def chunk(dd, i):
        return jnp.where(dd == 0, i, nt - 1 - i)

    def u_map(b, dd, i):
        return (base + b * nt + chunk(dd, i), col_blk)

    def prev_map(b, dd, i):
        return (jnp.maximum((row_off + b * t + chunk(dd, i) * tt) // hb - 1, 0), col_blk)

    def next_map(b, dd, i):
        return (jnp.minimum((row_off + b * t + (chunk(dd, i) + 1) * tt) // hb, ntot // hb - 1), col_blk)

    kern = functools.partial(_lru_kernel, tt=tt, nt=nt, t=t)
    dir_w = lambda b, dd, i: (dd, 0, 0, 0)
    dir_v = lambda b, dd, i: (dd, 0, 0)
    return pl.pallas_call(
        kern,
        grid=(batch, 2, nt),
        in_specs=[pl.BlockSpec((tt, c), u_map),
                  pl.BlockSpec((hb, c), prev_map),
                  pl.BlockSpec((hb, c), next_map),
                  pl.BlockSpec((CONV_W, c), lambda b, dd, i: (0, 0)),
                  pl.BlockSpec((1, c), lambda b, dd, i: (0, 0)),
                  pl.BlockSpec((1, nblk, LRU_BLOCK_W, LRU_BLOCK_W), dir_w),
                  pl.BlockSpec((1, 1, c), dir_v),
                  pl.BlockSpec((1, nblk, LRU_BLOCK_W, LRU_BLOCK_W), dir_w),
                  pl.BlockSpec((1, 1, c), dir_v),
                  pl.BlockSpec((1, 1, c), dir_v)],
        out_specs=pl.BlockSpec((1, nblk, tt, LRU_BLOCK_W), lambda b, dd, i: (dd, 0, b * nt + chunk(dd, i), 0)),
        out_shape=jax.ShapeDtypeStruct((2, nblk, batch * t, LRU_BLOCK_W), F32),
        scratch_shapes=[pltpu.VMEM((tt + 2 * SUBLANES, c), F32),
                        pltpu.VMEM((nblk, tt, LRU_BLOCK_W), F32),
                        pltpu.VMEM((nblk, tt, LRU_BLOCK_W), F32),
                        pltpu.VMEM((nblk, LRU_BLOCK_W), F32)],
        compiler_params=_cparams("arbitrary", "arbitrary", "arbitrary"),
        name="lru",
    )(z, z, z, cw, cb.reshape(1, c), wa, ba.reshape(2, 1, c), wx, bx.reshape(2, 1, c), lam.reshape(2, 1, c))


def _gelu_tanh(x):
    return 0.5 * x * (1.0 + jnp.tanh(0.7978845608028654 * (x + 0.044715 * (x * x * x))))


def _outproj_kernel(*refs, starts):
    ng = len(starts)
    groups = [refs[3 * k:3 * k + 3] for k in range(ng)]
    (g_ref, x_ref, ga_ref, gl_ref, wo_ref, gf_ref, wr_ref, br_ref,
     x1_ref, h2_ref, idx_ref, gate_ref) = refs[3 * ng:]
    i = pl.program_id(0)
    nblk = groups[0][1].shape[1]

    def read(k):
        attn_ref, hf_ref, hb_ref = groups[k]
        return (attn_ref[...].astype(F32),
                jnp.concatenate([hf_ref[0, n] + hb_ref[0, n] for n in range(nblk)], axis=-1))

    attn, h = read(0)
    for k in range(1, ng):
        attn_k, h_k = read(k)
        attn = jnp.where(i >= starts[k], attn_k, attn)
        h = jnp.where(i >= starts[k], h_k, h)
    an = _rms(attn, ga_ref[...])
    rn = _rms(h * _gelu_tanh(g_ref[...].astype(F32)), gl_ref[...])
    mixed = jnp.concatenate([an, rn], axis=-1).astype(BF16)
    x1 = x_ref[...] + jnp.dot(mixed, wo_ref[...], preferred_element_type=F32)
    x1_ref[...] = x1
    h2 = _rms(x1, gf_ref[...])
    h2_ref[...] = h2

    logits = jnp.dot(h2, wr_ref[...], preferred_element_type=F32, precision=lax.Precision.HIGHEST) + br_ref[...]
    ne = logits.shape[-1]
    eidx = lax.broadcasted_iota(jnp.int32, logits.shape, 1)
    kidx = lax.broadcasted_iota(jnp.int32, idx_ref.shape, 1)
    vals = jnp.zeros(gate_ref.shape, F32)
    idxs = jnp.zeros(idx_ref.shape, jnp.int32)
    cur = logits
    for k in range(TOP_K):
        m = jnp.max(cur, axis=-1, keepdims=True)
        sel = jnp.min(jnp.where(cur == m, eidx, ne), axis=-1, keepdims=True)
        vals = jnp.where(kidx == k, m, vals)
        idxs = jnp.where(kidx == k, sel, idxs)
        cur = jnp.where(eidx == sel, -jnp.inf, cur)
    e = jnp.exp(vals - vals[:, 0:1])
    gate_ref[...] = e / jnp.sum(e, axis=-1, keepdims=True)
    idx_ref[...] = idxs


def _outproj(attns, hs, z, x, ga, gl, wo, gf, wr, br, *, g_blk, tm):
    n, d = x.shape
    aw = attns[0].shape[1]
    nblk = hs[0].shape[1]
    c = nblk * LRU_BLOCK_W
    ne = wr.shape[1]
    const = lambda i: (0, 0)
    starts, group_specs, group_args = [], [], []
    s0 = 0
    for attn, h in zip(attns, hs):
        nb = attn.shape[0] // tm
        assert attn.shape[0] % tm == 0
        local = lambda i, s0=s0, nb=nb: jnp.clip(i - s0, 0, nb - 1)
        group_specs += [pl.BlockSpec((tm, aw), lambda i, f=local: (f(i), 0)),
                        pl.BlockSpec((1, nblk, tm, LRU_BLOCK_W), lambda i, f=local: (0, 0, f(i), 0)),
                        pl.BlockSpec((1, nblk, tm, LRU_BLOCK_W), lambda i, f=local: (1, 0, f(i), 0))]
        group_args += [attn, h, h]
        starts.append(s0)
        s0 += nb
    assert s0 == n // tm
    return pl.pallas_call(
        functools.partial(_outproj_kernel, starts=tuple(starts)),
        grid=(n // tm,),
        in_specs=group_specs + [
                  pl.BlockSpec((tm, c), lambda i: (i, g_blk)),
                  pl.BlockSpec((tm, d), lambda i: (i, 0)),
                  pl.BlockSpec((1, aw), const),
                  pl.BlockSpec((1, c), const),
                  pl.BlockSpec((d, d), const),
                  pl.BlockSpec((1, d), const),
                  pl.BlockSpec((d, ne), const),
                  pl.BlockSpec((1, ne), const)],
        out_specs=[pl.BlockSpec((tm, d), lambda i: (i, 0)),
                   pl.BlockSpec((tm, d), lambda i: (i, 0)),
                   pl.BlockSpec((tm, TOP_K), lambda i: (i, 0)),
                   pl.BlockSpec((tm, TOP_K), lambda i: (i, 0))],
        out_shape=[jax.ShapeDtypeStruct((n, d), F32),
                   jax.ShapeDtypeStruct((n, d), F32),
                   jax.ShapeDtypeStruct((n, TOP_K), jnp.int32),
                   jax.ShapeDtypeStruct((n, TOP_K), F32)],
        compiler_params=_cparams("arbitrary"),
        name="outproj",
    )(*group_args, z, x, ga.reshape(1, aw), gl.reshape(1, c), wo, gf.reshape(1, d), wr, br.reshape(1, ne))


def _moe_plan(top_idx, n_experts):
    n = top_idx.shape[0]
    m = n * TOP_K
    n_sb = -(-m // MOE_TM) + n_experts
    flat_e = top_idx.reshape(-1)
    order = jnp.argsort(flat_e).astype(jnp.int32)
    counts = jnp.sum((flat_e[:, None] == jnp.arange(n_experts)[None, :]).astype(jnp.int32), axis=0)
    start = jnp.cumsum(counts) - counts
    sb_per_e = (counts + MOE_TM - 1) // MOE_TM
    sb_end = jnp.cumsum(sb_per_e)
    sb = jnp.arange(n_sb, dtype=jnp.int32)
    sb_e = jnp.minimum(jnp.searchsorted(sb_end, sb, side="right"), n_experts - 1).astype(jnp.int32)
    local = sb - (sb_end - sb_per_e)[sb_e]
    valid = jnp.clip(counts[sb_e] - local * MOE_TM, 0, MOE_TM)
    valid = jnp.where(sb < sb_end[-1], valid, 0).astype(jnp.int32)
    r = jnp.arange(MOE_TM, dtype=jnp.int32)
    live = r[None, :] < valid[:, None]
    sorted_pos = jnp.clip(start[sb_e][:, None] + local[:, None] * MOE_TM + r[None, :], 0, m - 1)
    a = order[sorted_pos]
    src = jnp.where(live, a // TOP_K, 0)
    dst = jnp.where(live, (a % TOP_K) * n + a // TOP_K, 0)
    rows = jnp.concatenate([src, dst], axis=1).reshape(-1).astype(jnp.int32)
    return sb_e, valid, rows


def _wprep_kernel(w_ref, o_ref):
    grp = 2 * LANES
    r = lax.broadcasted_iota(jnp.int32, (grp, grp), 0)
    c = lax.broadcasted_iota(jnp.int32, (grp, grp), 1)
    perm = (r == jnp.where(c < LANES, 2 * c, 2 * (c - LANES) + 1)).astype(BF16)
    for g in range(w_ref.shape[2] // grp):
        w = w_ref[0, :, g * grp:(g + 1) * grp].astype(BF16)
        o_ref[0, :, g * grp:(g + 1) * grp] = jnp.dot(w, perm, preferred_element_type=F32).astype(BF16)


def _wprep(w, *, tn):
    ne, d, f2 = w.shape
    return pl.pallas_call(
        _wprep_kernel,
        grid=(ne, f2 // tn),
        in_specs=[pl.BlockSpec((1, d, tn), lambda e, j: (e, 0, j))],
        out_specs=pl.BlockSpec((1, d, tn), lambda e, j: (e, 0, j)),
        out_shape=jax.ShapeDtypeStruct((ne, d, f2), BF16),
        compiler_params=_cparams("arbitrary", "arbitrary"),
        name="wprep",
    )(w)


def _regroup_gate_up(b):
    lead = b.shape[:-1]
    return b.reshape(lead + (-1, LANES, 2)).swapaxes(-1, -2).reshape(lead + (-1,))


def _moe_kernel(sbe_ref, live_ref, rows_hbm, h_hbm, wgu_ref, bgu_ref, wd_ref, bd_ref, y_hbm,
                rows_smem, xg, xb, acc, sem_idx, sem_in, sem_out):
    s = pl.program_id(0)
    j = pl.program_id(1)
    nj = pl.num_programs(1)
    nlive = live_ref[s]
    nsub = (nlive + MOE_SUB - 1) // MOE_SUB

    def sub_rows(q):
        return pl.ds(pl.multiple_of(q * MOE_SUB, MOE_SUB), MOE_SUB)

    def start_rows(start_row):
        ngrp = nlive // MOE_ISSUE_UNROLL

        def group(g, c):
            for u in range(MOE_ISSUE_UNROLL):
                start_row(g * MOE_ISSUE_UNROLL + u)
            return c

        def single(row, c):
            start_row(row)
            return c

        lax.fori_loop(0, ngrp, group, 0)
        lax.fori_loop(ngrp * MOE_ISSUE_UNROLL, nlive, single, 0)

    def wait_rows(src, dst, sem):
        def block(q, c):
            pltpu.make_async_copy(src.at[pl.ds(0, MOE_SUB), :], dst.at[pl.ds(0, MOE_SUB), :], sem).wait()
            return c

        def row(q, c):
            pltpu.make_async_copy(src.at[pl.ds(0, 1), :], dst.at[pl.ds(0, 1), :], sem).wait()
            return c

        nfull = nlive // MOE_SUB
        lax.fori_loop(0, nfull, block, 0)
        lax.fori_loop(0, nlive - nfull * MOE_SUB, row, 0)

    @pl.when(jnp.logical_and(s == 0, j == 0))
    def _():
        xg[...] = jnp.zeros_like(xg)

    @pl.when(jnp.logical_and(nsub > 0, j == 0))
    def _gather():
        cp = pltpu.make_async_copy(rows_hbm.at[pl.ds(pl.multiple_of(s * 2 * MOE_TM, 2 * MOE_TM), 2 * MOE_TM)],
                                   rows_smem, sem_idx)
        cp.start()
        cp.wait()

        def gather_row(row):
            tok = rows_smem[row]
            pltpu.make_async_copy(h_hbm.at[pl.ds(tok, 1), :], xg.at[pl.ds(row, 1), :], sem_in).start()

        start_rows(gather_row)
        wait_rows(h_hbm, xg, sem_in)

        def cast(q, c):
            xb[sub_rows(q), :] = xg[sub_rows(q), :].astype(BF16)
            acc[sub_rows(q), :] = jnp.broadcast_to(bd_ref[0], (MOE_SUB, acc.shape[1]))
            return c

        lax.fori_loop(0, nsub, cast, 0)

    def compute(q, c):
        x = xb[sub_rows(q), :]
        hgu = jnp.dot(x, wgu_ref[0], preferred_element_type=F32) + bgu_ref[0]
        nchunk = hgu.shape[1] // LANES
        gate = jnp.concatenate([hgu[:, b * LANES:(b + 1) * LANES] for b in range(0, nchunk, 2)], axis=-1)
        up = jnp.concatenate([hgu[:, b * LANES:(b + 1) * LANES] for b in range(1, nchunk, 2)], axis=-1)
        gate = jnp.minimum(gate, SWIGLU_LIMIT)
        up = jnp.clip(up, -SWIGLU_LIMIT, SWIGLU_LIMIT)
        act = (up + 1.0) * (gate * jax.nn.sigmoid(SWIGLU_ALPHA * gate))
        acc[sub_rows(q), :] += jnp.dot(act.astype(BF16), wd_ref[0], preferred_element_type=F32)
        return c

    lax.fori_loop(0, nsub, compute, 0)

    @pl.when(jnp.logical_and(nsub > 0, j == nj - 1))
    def _scatter():
        def scatter_row(row):
            dst = rows_smem[MOE_TM + row]
            pltpu.make_async_copy(acc.at[pl.ds(row, 1), :], y_hbm.at[pl.ds(dst, 1), :], sem_out).start()

        start_rows(scatter_row)
        wait_rows(acc, y_hbm, sem_out)


def _moe(h2, sb_e, live, rows, wgu, bgu, wdn, bdn):
    n, d = h2.shape
    ne, f = wdn.shape[0], wdn.shape[1]
    nj = f // MOE_TF
    n_sb = sb_e.shape[0]

    def ff(s, j, live_ref):
        return jnp.where(live_ref[s] > 0, j, nj - 1)

    grid_spec = pltpu.PrefetchScalarGridSpec(
        num_scalar_prefetch=2,
        grid=(n_sb, nj),
        in_specs=[pl.BlockSpec(memory_space=pl.ANY),
                  pl.BlockSpec(memory_space=pl.ANY),
                  pl.BlockSpec((1, d, 2 * MOE_TF), lambda s, j, e, ns: (e[s], 0, ff(s, j, ns))),
                  pl.BlockSpec((1, 1, 2 * MOE_TF), lambda s, j, e, ns: (e[s], 0, ff(s, j, ns))),
                  pl.BlockSpec((1, MOE_TF, d), lambda s, j, e, ns: (e[s], ff(s, j, ns), 0)),
                  pl.BlockSpec((1, 1, d), lambda s, j, e, ns: (e[s], 0, 0))],
        out_specs=pl.BlockSpec(memory_space=pl.ANY),
        scratch_shapes=[pltpu.SMEM((2 * MOE_TM,), jnp.int32),
                        pltpu.VMEM((MOE_TM, d), F32),
                        pltpu.VMEM((MOE_TM, d), BF16),
                        pltpu.VMEM((MOE_TM, d), F32),
                        pltpu.SemaphoreType.DMA(()),
                        pltpu.SemaphoreType.DMA(()),
                        pltpu.SemaphoreType.DMA(())])
    return pl.pallas_call(
        _moe_kernel,
        grid_spec=grid_spec,
        out_shape=jax.ShapeDtypeStruct((TOP_K * n, d), F32),
        compiler_params=_cparams("arbitrary", "arbitrary"),
        name="moe",
    )(sb_e, live, rows, h2, wgu, bgu, wdn, bdn)


def _combine_kernel(x_ref, y0_ref, y1_ref, y2_ref, y3_ref, gate_ref, gfin_ref, *o_refs, final, starts):
    g = gate_ref[...]
    x = x_ref[...]
    for k, y_ref in enumerate((y0_ref, y1_ref, y2_ref, y3_ref)):
        x = x + g[:, k:k + 1] * y_ref[...]
    x = _rms(x, gfin_ref[...]) if final else x
    if len(o_refs) == 1:
        o_refs[0][...] = x
        return
    i = pl.program_id(0)
    bounds = list(starts) + [pl.num_programs(0)]
    for k, o_ref in enumerate(o_refs):
        @pl.when(jnp.logical_and(i >= bounds[k], i < bounds[k + 1]))
        def _(o_ref=o_ref):
            o_ref[...] = x


def _combine(x1, y, gates, gfin, *, final, tm, split=None):
    n, d = x1.shape
    nb = n // tm
    y_spec = lambda k: pl.BlockSpec((tm, d), lambda i: (k * nb + i, 0))
    if split is None:
        starts = (0,)
        out_specs = pl.BlockSpec((tm, d), lambda i: (i, 0))
        out_shape = jax.ShapeDtypeStruct((n, d), F32)
    else:
        starts, out_specs, out_shape, s0 = [], [], [], 0
        for nk in split:
            assert nk % tm == 0
            out_specs.append(pl.BlockSpec((tm, d), lambda i, s0=s0, nbk=nk // tm: (jnp.clip(i - s0, 0, nbk - 1), 0)))
            out_shape.append(jax.ShapeDtypeStruct((nk, d), F32))
            starts.append(s0)
            s0 += nk // tm
        starts = tuple(starts)
    return pl.pallas_call(
        functools.partial(_combine_kernel, final=final, starts=starts),
        grid=(nb,),
        in_specs=[pl.BlockSpec((tm, d), lambda i: (i, 0)), y_spec(0), y_spec(1), y_spec(2), y_spec(3),
                  pl.BlockSpec((tm, TOP_K), lambda i: (i, 0)),
                  pl.BlockSpec((1, d), lambda i: (0, 0))],
        out_specs=out_specs,
        out_shape=out_shape,
        compiler_params=_cparams("arbitrary"),
        name="combine",
    )(x1, y, y, y, y, gates, gfin.reshape(1, d))


def _pick(n, *cands):
    for c in cands:
        if n % c == 0:
            return c
    raise ValueError(f"no tile for {n}")


def kernel(x_prompt, x_sample, g_mix, w_in, rpb, conv_w, conv_b, w_rg_a, b_rg_a, w_rg_x, b_rg_x, rg_lambda,
           g_attn_out, g_lru_out, w_out, g_ffn, w_router, b_router, w_gate_up, b_gate_up, w_down, b_down, g_final):
    depth = w_in.shape[0]
    d = x_prompt.shape[-1]
    attn_w = rpb.shape[1] * HEAD_DIM
    lru_w = conv_w.shape[-1]
    ne = w_router.shape[-1]
    f = w_down.shape[2]
    seqs = []
    off = 0
    for xs in (x_prompt, x_sample):
        seqs.append((off, xs.shape[0], xs.shape[1]))
        off += xs.shape[0] * xs.shape[1]
    n = off
    x = jnp.concatenate([x_prompt.reshape(-1, d), x_sample.reshape(-1, d)], axis=0)
    assert attn_w % LANES == 0 and lru_w == attn_w, "column blocks of z assume equal attention / recurrent widths"

    for l in range(depth):
        z = _inproj(x, g_mix[l], w_in[l].astype(BF16), tm=_pick(n, 1024, 512, 256), tn=_pick(w_in.shape[2], 1024, 512, 256))
        bias = _na_bias_table(rpb[l])
        wa, wx = w_rg_a[l].astype(BF16), w_rg_x[l].astype(BF16)
        attn, h = [], []
        for (o, b, t) in seqs:
            attn.append(_na(z, bias, row_off=o, batch=b, t=t, attn_w=attn_w, tq=_pick(t, 512, 256, 128, 64)))
            lru_args = (z, conv_w[l], conv_b[l], wa, b_rg_a[l], wx, b_rg_x[l], rg_lambda[l])
            if b == SUBLANES and n % t == 0 and o % (b * t) == 0:
                h.append(_lru_batched(*lru_args, row_off=o, t=t, col_blk=3, tt=_pick(t, 128, 64)))
            else:
                h.append(_lru(*lru_args, row_off=o, batch=b, t=t, col_blk=3, tt=_pick(t, 512, 256, 128, 64)))
        x1, h2, top_idx, gates = _outproj(attn, h, z, x, g_attn_out[l], g_lru_out[l], w_out[l].astype(BF16),
                                          g_ffn[l], w_router[l], b_router[l], g_blk=4, tm=_pick(n, 256, 128))
        sb_e, live, rows = _moe_plan(top_idx, ne)
        wgu = _wprep(w_gate_up[l], tn=_pick(2 * f, 1024, 512, 256))
        bgu = _regroup_gate_up(b_gate_up[l]).reshape(ne, 1, 2 * f)
        y = _moe(h2, sb_e, live, rows, wgu, bgu, w_down[l].astype(BF16), b_down[l].reshape(ne, 1, d))
        last = l == depth - 1
        x = _combine(x1, y, gates, g_final, final=last, tm=_pick(n, 256, 128),
                     split=tuple(b * t for (_, b, t) in seqs) if last else None)

    return x[0].reshape(x_prompt.shape), x[1].reshape(x_sample.shape)
```

```python
import functools

import jax
import jax.numpy as jnp
from jax import lax
from jax.experimental import pallas as pl
from jax.experimental.pallas import tpu as pltpu

F32 = jnp.float32
BF16 = jnp.bfloat16

GRID_W = 64
HEAD_DIM = 64
NA_ROWS = 8
NA_COLS = 16
CONV_W = 4
RG_C = 8.0
LRU_BLOCK_W = 128
TOP_K = 4
SWIGLU_LIMIT = 7.0
SWIGLU_ALPHA = 1.702
NORM_EPS = 1e-5
MASK_BIAS = -1e30

LANES = 128
SUBLANES = 8
BF16_SUBLANES = 16
VMEM_LIMIT = 56 * 1024 * 1024

MOE_SUB = 256
MOE_SUBS_PER_BLOCK = 4
MOE_TM = MOE_SUB * MOE_SUBS_PER_BLOCK
MOE_TF = 512
MOE_ISSUE_UNROLL = 4


def _cparams(*sem):
    return pltpu.CompilerParams(dimension_semantics=sem, vmem_limit_bytes=VMEM_LIMIT)


def _rms(x, g):
    return x * lax.rsqrt(jnp.mean(x * x, axis=-1, keepdims=True) + NORM_EPS) * g


def _inproj_kernel(x_ref, g_ref, w_ref, o_ref, xn_ref):
    @pl.when(pl.program_id(1) == 0)
    def _():
        xn_ref[...] = _rms(x_ref[...], g_ref[...]).astype(BF16)

    o_ref[...] = jnp.dot(xn_ref[...], w_ref[...], preferred_element_type=F32).astype(o_ref.dtype)


def _inproj(x, g, w, *, tm, tn):
    n, d = x.shape
    wn = w.shape[1]
    return pl.pallas_call(
        _inproj_kernel,
        grid=(n // tm, wn // tn),
        in_specs=[pl.BlockSpec((tm, d), lambda i, j: (i, 0)),
                  pl.BlockSpec((1, d), lambda i, j: (0, 0)),
                  pl.BlockSpec((d, tn), lambda i, j: (0, j))],
        out_specs=pl.BlockSpec((tm, tn), lambda i, j: (i, j)),
        out_shape=jax.ShapeDtypeStruct((n, wn), BF16),
        scratch_shapes=[pltpu.VMEM((tm, d), BF16)],
        compiler_params=_cparams("arbitrary", "arbitrary"),
        name="inproj",
    )(x, g.reshape(1, d), w)


def _na_bias_table(rpb):
    h = rpb.shape[0]
    c = jnp.arange(GRID_W)
    col_start = jnp.clip(c - NA_COLS // 2, 0, GRID_W - NA_COLS)
    j = jnp.arange(GRID_W)
    inwin = (j[None, :] >= col_start[:, None]) & (j[None, :] < col_start[:, None] + NA_COLS)
    dc = jnp.clip(j[None, :] - c[:, None] + (NA_COLS - 1), 0, 2 * NA_COLS - 2)
    delta = jnp.arange(NA_ROWS)
    kk = jnp.arange(NA_ROWS)
    dr = kk[None, :] - delta[:, None] + (NA_ROWS - 1)
    row_sel = jax.nn.one_hot(dr, 2 * NA_ROWS - 1, dtype=F32)
    col_sel = jax.nn.one_hot(dc, 2 * NA_COLS - 1, dtype=F32)
    t = jnp.einsum("dka,hab,cjb->hdkcj", row_sel, rpb.astype(F32), col_sel, precision=lax.Precision.HIGHEST)
    t = jnp.where(inwin[None, None, None], t, MASK_BIAS)
    t = t.transpose(0, 1, 3, 2, 4).reshape(h // 2, 2, NA_ROWS, GRID_W, NA_ROWS * GRID_W)
    return t.transpose(0, 2, 1, 3, 4).reshape(h // 2, NA_ROWS, 2 * GRID_W, NA_ROWS * GRID_W)


def _na_kernel(q_ref, k_ref, v_ref, b_ref, o_ref, *, rows, rows_per_tile):
    ti = pl.program_id(2)
    lane = lax.broadcasted_iota(jnp.int32, (GRID_W, 2 * HEAD_DIM), 1)
    first_head = lane < HEAD_DIM
    win = NA_ROWS * GRID_W

    def body(rr, carry):
        r = ti * rows_per_tile + rr
        r0 = jnp.clip(r - NA_ROWS // 2, 0, rows - NA_ROWS)
        q = q_ref[pl.ds(pl.multiple_of(rr * GRID_W, GRID_W), GRID_W), :] * (HEAD_DIM ** -0.5)
        zero = jnp.zeros_like(q)
        qm = jnp.concatenate([jnp.where(first_head, q, zero), jnp.where(first_head, zero, q)], axis=0)
        ks = pl.multiple_of(r0 * GRID_W, GRID_W)
        kw = k_ref[pl.ds(ks, win), :]
        vw = v_ref[pl.ds(ks, win), :]
        s = lax.dot_general(qm, kw, (((1,), (1,)), ((), ())), preferred_element_type=F32)
        s = s + b_ref[0, r - r0]
        m = jnp.max(s, axis=-1, keepdims=True)
        p = jnp.exp(s - m)
        l = jnp.sum(p, axis=-1, keepdims=True)
        o = jnp.dot(p.astype(BF16), vw, preferred_element_type=F32) / l
        out = jnp.where(first_head, o[:GRID_W], o[GRID_W:])
        o_ref[pl.ds(pl.multiple_of(rr * GRID_W, GRID_W), GRID_W), :] = out.astype(o_ref.dtype)
        return carry

    lax.fori_loop(0, rows_per_tile, body, 0, unroll=True)


def _na(z, bias, *, row_off, batch, t, attn_w, tq):
    rows = t // GRID_W
    assert rows >= NA_ROWS and t % tq == 0 and row_off % t == 0 and tq % GRID_W == 0
    hp = attn_w // LANES
    nq = t // tq
    qoff, koff = row_off // tq, row_off // t
    kern = functools.partial(_na_kernel, rows=rows, rows_per_tile=tq // GRID_W)
    return pl.pallas_call(
        kern,
        grid=(batch, hp, nq),
        in_specs=[pl.BlockSpec((tq, LANES), lambda b, h, i: (qoff + b * nq + i, h)),
                  pl.BlockSpec((t, LANES), lambda b, h, i: (koff + b, hp + h)),
                  pl.BlockSpec((t, LANES), lambda b, h, i: (koff + b, 2 * hp + h)),
                  pl.BlockSpec((1,) + bias.shape[1:], lambda b, h, i: (h, 0, 0, 0))],
        out_specs=pl.BlockSpec((tq, LANES), lambda b, h, i: (b * nq + i, h)),
        out_shape=jax.ShapeDtypeStruct((batch * t, attn_w), BF16),
        compiler_params=_cparams("arbitrary", "arbitrary", "arbitrary"),
        name="na",
    )(z, z, z, bias)


def _softplus(x):
    return jnp.maximum(x, 0.0) + jnp.log(1.0 + jnp.exp(-jnp.abs(x)))


def _lru_coeffs(u, prev, nxt, ext, refs, *, d, ci, tt, nt, t, store):
    cw_ref, cb_ref, wa_ref, ba_ref, wx_ref, bx_ref, lam_ref = refs
    halo = SUBLANES
    ext[pl.ds(0, halo), :] = jnp.where(ci == 0, 0.0, prev)
    ext[pl.ds(halo, tt), :] = u
    ext[pl.ds(halo + tt, halo), :] = jnp.where(ci == nt - 1, 0.0, nxt)
    c = cb_ref[...]
    for j in range(CONV_W):
        c = c + cw_ref[pl.ds(j, 1), :] * ext[pl.ds(halo - CONV_W // 2 + j, tt), :]
    cbf = c.astype(BF16)

    tpos = ci * tt + lax.broadcasted_iota(jnp.int32, (tt, 1), 0)
    is_first = tpos == jnp.where(d == 0, 0, t - 1)
    sp = _softplus(-lam_ref[0])
    for n in range(wa_ref.shape[1]):
        sl = slice(n * LRU_BLOCK_W, (n + 1) * LRU_BLOCK_W)
        cn = cbf[:, sl]
        r = jax.nn.sigmoid(jnp.dot(cn, wa_ref[0, n], preferred_element_type=F32) + ba_ref[0, :, sl])
        ig = jax.nn.sigmoid(jnp.dot(cn, wx_ref[0, n], preferred_element_type=F32) + bx_ref[0, :, sl])
        a = jnp.exp(-RG_C * r * sp[:, sl])
        mult = jnp.where(is_first, 1.0, jnp.sqrt(1.0 - a * a))
        store(n, a, mult * (ig * c[:, sl]))


def _lru_batched_kernel(u_ref, up_ref, un_ref, cw_ref, cb_ref, wa_ref, ba_ref, wx_ref, bx_ref, lam_ref, o_ref,
                        ext, a_scr, b_scr, carry, *, tt, nt, t):
    d = pl.program_id(0)
    i = pl.program_id(1)
    ci = jnp.where(d == 0, i, nt - 1 - i)
    nblk = a_scr.shape[0]
    refs = (cw_ref, cb_ref, wa_ref, ba_ref, wx_ref, bx_ref, lam_ref)

    def per_sequence(b, c0):
        def store(n, a, bb):
            a_scr[n, :, b, :] = a
            b_scr[n, :, b, :] = bb

        _lru_coeffs(u_ref[b].astype(F32), up_ref[b].astype(F32)[BF16_SUBLANES - SUBLANES:, :],
                    un_ref[b].astype(F32)[:SUBLANES, :], ext, refs, d=d, ci=ci, tt=tt, nt=nt, t=t, store=store)
        return c0

    lax.fori_loop(0, SUBLANES, per_sequence, 0)

    @pl.when(i == 0)
    def _():
        carry[...] = jnp.zeros_like(carry)

    def step(jj, hs):
        j = jnp.where(d == 0, jj, tt - 1 - jj)
        new = []
        for n in range(nblk):
            h = a_scr[n, j] * hs[n] + b_scr[n, j]
            b_scr[n, j] = h
            new.append(h)
        return tuple(new)

    hs = lax.fori_loop(0, tt, step, tuple(carry[n] for n in range(nblk)))
    for n in range(nblk):
        carry[n] = hs[n]

    def write(b, c0):
        for n in range(nblk):
            o_ref[0, n, b] = b_scr[n, :, b, :]
        return c0

    lax.fori_loop(0, SUBLANES, write, 0)


def _lru_batched(z, cw, cb, wa, ba, wx, bx, lam, *, row_off, t, col_blk, tt):
    c = cw.shape[1]
    nblk = c // LRU_BLOCK_W
    nt = t // tt
    ntot = z.shape[0]
    nb = SUBLANES
    assert t % tt == 0 and ntot % t == 0 and row_off % (nb * t) == 0 and tt % BF16_SUBLANES == 0
    z3 = z.reshape(ntot // t, t, z.shape[1])
    g = row_off // (nb * t)
    hb = BF16_SUBLANES

    def chunk(dd, i):
        return jnp.where(dd == 0, i, nt - 1 - i)

    kern = functools.partial(_lru_batched_kernel, tt=tt, nt=nt, t=t)
    dir_w = lambda dd, i: (dd, 0, 0, 0)
    dir_v = lambda dd, i: (dd, 0, 0)
    out = pl.pallas_call(
        kern,
        grid=(2, nt),
        in_specs=[pl.BlockSpec((nb, tt, c), lambda dd, i: (g, chunk(dd, i), col_blk)),
                  pl.BlockSpec((nb, hb, c), lambda dd, i: (g, jnp.maximum(chunk(dd, i) * (tt // hb) - 1, 0), col_blk)),
                  pl.BlockSpec((nb, hb, c), lambda dd, i: (g, jnp.minimum((chunk(dd, i) + 1) * (tt // hb), t // hb - 1), col_blk)),
                  pl.BlockSpec((CONV_W, c), lambda dd, i: (0, 0)),
                  pl.BlockSpec((1, c), lambda dd, i: (0, 0)),
                  pl.BlockSpec((1, nblk, LRU_BLOCK_W, LRU_BLOCK_W), dir_w),
                  pl.BlockSpec((1, 1, c), dir_v),
                  pl.BlockSpec((1, nblk, LRU_BLOCK_W, LRU_BLOCK_W), dir_w),
                  pl.BlockSpec((1, 1, c), dir_v),
                  pl.BlockSpec((1, 1, c), dir_v)],
        out_specs=pl.BlockSpec((1, nblk, nb, tt, LRU_BLOCK_W), lambda dd, i: (dd, 0, 0, chunk(dd, i), 0)),
        out_shape=jax.ShapeDtypeStruct((2, nblk, nb, t, LRU_BLOCK_W), F32),
        scratch_shapes=[pltpu.VMEM((tt + 2 * SUBLANES, c), F32),
                        pltpu.VMEM((nblk, tt, nb, LRU_BLOCK_W), F32),
                        pltpu.VMEM((nblk, tt, nb, LRU_BLOCK_W), F32),
                        pltpu.VMEM((nblk, nb, LRU_BLOCK_W), F32)],
        compiler_params=_cparams("arbitrary", "arbitrary"),
        name="lru_batched",
    )(z3, z3, z3, cw, cb.reshape(1, c), wa, ba.reshape(2, 1, c), wx, bx.reshape(2, 1, c), lam.reshape(2, 1, c))
    return out.reshape(2, nblk, nb * t, LRU_BLOCK_W)


def _lru_kernel(u_ref, up_ref, un_ref, cw_ref, cb_ref, wa_ref, ba_ref, wx_ref, bx_ref, lam_ref, o_ref,
                ext, a_scr, b_scr, carry, *, tt, nt, t):
    d = pl.program_id(1)
    i = pl.program_id(2)
    ci = jnp.where(d == 0, i, nt - 1 - i)
    nblk = a_scr.shape[0]
    seg = tt // SUBLANES

    def store(n, a, bb):
        a_scr[n] = a
        b_scr[n] = bb

    _lru_coeffs(u_ref[...].astype(F32), up_ref[...].astype(F32)[BF16_SUBLANES - SUBLANES:, :],
                un_ref[...].astype(F32)[:SUBLANES, :], ext,
                (cw_ref, cb_ref, wa_ref, ba_ref, wx_ref, bx_ref, lam_ref), d=d, ci=ci, tt=tt, nt=nt, t=t, store=store)

    @pl.when(i == 0)
    def _():
        carry[...] = jnp.zeros_like(carry)

    def step_index(jj):
        return jnp.where(d == 0, jj, seg - 1 - jj)

    def load(ref, n, j):
        return ref[n, pl.ds(j, SUBLANES, stride=seg), :]

    def local_step(jj, hp):
        j = step_index(jj)
        hs, ps = hp
        new_h, new_p = [], []
        for n in range(nblk):
            a = load(a_scr, n, j)
            new_h.append(a * hs[n] + load(b_scr, n, j))
            new_p.append(a * ps[n])
        return tuple(new_h), tuple(new_p)

    zeros = tuple(jnp.zeros((SUBLANES, LRU_BLOCK_W), F32) for _ in range(nblk))
    ones = tuple(jnp.ones((SUBLANES, LRU_BLOCK_W), F32) for _ in range(nblk))
    h_end, p_end = lax.fori_loop(0, seg, local_step, (zeros, ones))

    cin, cout = [], []
    for n in range(nblk):
        c0 = carry[pl.ds(n, 1), :]
        fwd, cur = [], c0
        for s in range(SUBLANES):
            fwd.append(cur)
            cur = h_end[n][s:s + 1] + p_end[n][s:s + 1] * cur
        fwd_out = cur
        bwd, cur = [None] * SUBLANES, c0
        for s in reversed(range(SUBLANES)):
            bwd[s] = cur
            cur = h_end[n][s:s + 1] + p_end[n][s:s + 1] * cur
        bwd_out = cur
        cin.append(jnp.where(d == 0, jnp.concatenate(fwd, axis=0), jnp.concatenate(bwd, axis=0)))
        cout.append(jnp.where(d == 0, fwd_out, bwd_out))

    def final_step(jj, hs):
        j = step_index(jj)
        new_h = []
        for n in range(nblk):
            h = load(a_scr, n, j) * hs[n] + load(b_scr, n, j)
            o_ref[0, n, pl.ds(j, SUBLANES, stride=seg), :] = h
            new_h.append(h)
        return tuple(new_h)

    lax.fori_loop(0, seg, final_step, tuple(cin))
    for n in range(nblk):
        carry[pl.ds(n, 1), :] = cout[n]


def _lru(z, cw, cb, wa, ba, wx, bx, lam, *, row_off, batch, t, col_blk, tt):
    c = cw.shape[1]
    nblk = c // LRU_BLOCK_W
    nt = t // tt
    ntot = z.shape[0]
    assert t % tt == 0 and row_off % tt == 0 and tt % (SUBLANES * SUBLANES) == 0
    hb = BF16_SUBLANES
    base = row_off // tt

    def chunk(dd, i):
        return jnp.where(dd == 0, i, nt - 1 - i)

    def u_map(b, dd, i):
        return (base + b * nt + chunk(dd, i), col_blk)

    def prev_map(b, dd, i):
        return (jnp.maximum((row_off + b * t + chunk(dd, i) * tt) // hb - 1, 0), col_blk)

    def next_map(b, dd, i):
        return (jnp.minimum((row_off + b * t + (chunk(dd, i) + 1) * tt) // hb, ntot // hb - 1), col_blk)

    kern = functools.partial(_lru_kernel, tt=tt, nt=nt, t=t)
    dir_w = lambda b, dd, i: (dd, 0, 0, 0)
    dir_v = lambda b, dd, i: (dd, 0, 0)
    return pl.pallas_call(
        kern,
        grid=(batch, 2, nt),
        in_specs=[pl.BlockSpec((tt, c), u_map),
                  pl.BlockSpec((hb, c), prev_map),
                  pl.BlockSpec((hb, c), next_map),
                  pl.BlockSpec((CONV_W, c), lambda b, dd, i: (0, 0)),
                  pl.BlockSpec((1, c), lambda b, dd, i: (0, 0)),
                  pl.BlockSpec((1, nblk, LRU_BLOCK_W, LRU_BLOCK_W), dir_w),
                  pl.BlockSpec((1, 1, c), dir_v),
                  pl.BlockSpec((1, nblk, LRU_BLOCK_W, LRU_BLOCK_W), dir_w),
                  pl.BlockSpec((1, 1, c), dir_v),
                  pl.BlockSpec((1, 1, c), dir_v)],
        out_specs=pl.BlockSpec((1, nblk, tt, LRU_BLOCK_W), lambda b, dd, i: (dd, 0, b * nt + chunk(dd, i), 0)),
        out_shape=jax.ShapeDtypeStruct((2, nblk, batch * t, LRU_BLOCK_W), F32),
        scratch_shapes=[pltpu.VMEM((tt + 2 * SUBLANES, c), F32),
                        pltpu.VMEM((nblk, tt, LRU_BLOCK_W), F32),
                        pltpu.VMEM((nblk, tt, LRU_BLOCK_W), F32),
                        pltpu.VMEM((nblk, LRU_BLOCK_W), F32)],
        compiler_params=_cparams("arbitrary", "arbitrary", "arbitrary"),
        name="lru",
    )(z, z, z, cw, cb.reshape(1, c), wa, ba.reshape(2, 1, c), wx, bx.reshape(2, 1, c), lam.reshape(2, 1, c))


def _gelu_tanh(x):
    return 0.5 * x * (1.0 + jnp.tanh(0.7978845608028654 * (x + 0.044715 * (x * x * x))))


def _outproj_kernel(*refs, starts):
    ng = len(starts)
    groups = [refs[3 * k:3 * k + 3] for k in range(ng)]
    (g_ref, x_ref, ga_ref, gl_ref, wo_ref, gf_ref, wr_ref, br_ref,
     x1_ref, h2_ref, idx_ref, gate_ref) = refs[3 * ng:]
    i = pl.program_id(0)
    nblk = groups[0][1].shape[1]

    def read(k):
        attn_ref, hf_ref, hb_ref = groups[k]
        return (attn_ref[...].astype(F32),
                jnp.concatenate([hf_ref[0, n] + hb_ref[0, n] for n in range(nblk)], axis=-1))

    attn, h = read(0)
    for k in range(1, ng):
        attn_k, h_k = read(k)
        attn = jnp.where(i >= starts[k], attn_k, attn)
        h = jnp.where(i >= starts[k], h_k, h)
    an = _rms(attn, ga_ref[...])
    rn = _rms(h * _gelu_tanh(g_ref[...].astype(F32)), gl_ref[...])
    mixed = jnp.concatenate([an, rn], axis=-1).astype(BF16)
    x1 = x_ref[...] + jnp.dot(mixed, wo_ref[...], preferred_element_type=F32)
    x1_ref[...] = x1
    h2 = _rms(x1, gf_ref[...])
    h2_ref[...] = h2

    logits = jnp.dot(h2, wr_ref[...], preferred_element_type=F32, precision=lax.Precision.HIGHEST) + br_ref[...]
    ne = logits.shape[-1]
    eidx = lax.broadcasted_iota(jnp.int32, logits.shape, 1)
    kidx = lax.broadcasted_iota(jnp.int32, idx_ref.shape, 1)
    vals = jnp.zeros(gate_ref.shape, F32)
    idxs = jnp.zeros(idx_ref.shape, jnp.int32)
    cur = logits
    for k in range(TOP_K):
        m = jnp.max(cur, axis=-1, keepdims=True)
        sel = jnp.min(jnp.where(cur == m, eidx, ne), axis=-1, keepdims=True)
        vals = jnp.where(kidx == k, m, vals)
        idxs = jnp.where(kidx == k, sel, idxs)
        cur = jnp.where(eidx == sel, -jnp.inf, cur)
    e = jnp.exp(vals - vals[:, 0:1])
    gate_ref[...] = e / jnp.sum(e, axis=-1, keepdims=True)
    idx_ref[...] = idxs


def _outproj(attns, hs, z, x, ga, gl, wo, gf, wr, br, *, g_blk, tm):
    n, d = x.shape
    aw = attns[0].shape[1]
    nblk = hs[0].shape[1]
    c = nblk * LRU_BLOCK_W
    ne = wr.shape[1]
    const = lambda i: (0, 0)
    starts, group_specs, group_args = [], [], []
    s0 = 0
    for attn, h in zip(attns, hs):
        nb = attn.shape[0] // tm
        assert attn.shape[0] % tm == 0
        local = lambda i, s0=s0, nb=nb: jnp.clip(i - s0, 0, nb - 1)
        group_specs += [pl.BlockSpec((tm, aw), lambda i, f=local: (f(i), 0)),
                        pl.BlockSpec((1, nblk, tm, LRU_BLOCK_W), lambda i, f=local: (0, 0, f(i), 0)),
                        pl.BlockSpec((1, nblk, tm, LRU_BLOCK_W), lambda i, f=local: (1, 0, f(i), 0))]
        group_args += [attn, h, h]
        starts.append(s0)
        s0 += nb
    assert s0 == n // tm
    return pl.pallas_call(
        functools.partial(_outproj_kernel, starts=tuple(starts)),
        grid=(n // tm,),
        in_specs=group_specs + [
                  pl.BlockSpec((tm, c), lambda i: (i, g_blk)),
                  pl.BlockSpec((tm, d), lambda i: (i, 0)),
                  pl.BlockSpec((1, aw), const),
                  pl.BlockSpec((1, c), const),
                  pl.BlockSpec((d, d), const),
                  pl.BlockSpec((1, d), const),
                  pl.BlockSpec((d, ne), const),
                  pl.BlockSpec((1, ne), const)],
        out_specs=[pl.BlockSpec((tm, d), lambda i: (i, 0)),
                   pl.BlockSpec((tm, d), lambda i: (i, 0)),
                   pl.BlockSpec((tm, TOP_K), lambda i: (i, 0)),
                   pl.BlockSpec((tm, TOP_K), lambda i: (i, 0))],
        out_shape=[jax.ShapeDtypeStruct((n, d), F32),
                   jax.ShapeDtypeStruct((n, d), F32),
                   jax.ShapeDtypeStruct((n, TOP_K), jnp.int32),
                   jax.ShapeDtypeStruct((n, TOP_K), F32)],
        compiler_params=_cparams("arbitrary"),
        name="outproj",
    )(*group_args, z, x, ga.reshape(1, aw), gl.reshape(1, c), wo, gf.reshape(1, d), wr, br.reshape(1, ne))


def _moe_plan(top_idx, n_experts):
    n = top_idx.shape[0]
    m = n * TOP_K
    n_sb = -(-m // MOE_TM) + n_experts + 1
    flat_e = top_idx.reshape(-1)
    order = jnp.argsort(flat_e).astype(jnp.int32)
    counts = jnp.sum((flat_e[:, None] == jnp.arange(n_experts)[None, :]).astype(jnp.int32), axis=0)
    start = jnp.cumsum(counts) - counts
    sb_per_e = (counts + MOE_TM - 1) // MOE_TM
    sb_end = jnp.cumsum(sb_per_e)
    sb = jnp.arange(n_sb, dtype=jnp.int32)
    sb_e = jnp.minimum(jnp.searchsorted(sb_end, sb, side="right"), n_experts - 1).astype(jnp.int32)
    local = sb - (sb_end - sb_per_e)[sb_e]
    valid = jnp.clip(counts[sb_e] - local * MOE_TM, 0, MOE_TM)
    valid = jnp.where(sb < sb_end[-1], valid, 0).astype(jnp.int32)
    r = jnp.arange(MOE_TM, dtype=jnp.int32)
    live = r[None, :] < valid[:, None]
    sorted_pos = jnp.clip(start[sb_e][:, None] + local[:, None] * MOE_TM + r[None, :], 0, m - 1)
    a = order[sorted_pos]
    src = jnp.where(live, a // TOP_K, 0)
    dst = jnp.where(live, (a % TOP_K) * n + a // TOP_K, m + r[None, :])
    rows = jnp.concatenate([src, dst], axis=1).reshape(-1).astype(jnp.int32)
    return sb_e, valid, rows


def _wprep_kernel(w_ref, o_ref):
    grp = 2 * LANES
    r = lax.broadcasted_iota(jnp.int32, (grp, grp), 0)
    c = lax.broadcasted_iota(jnp.int32, (grp, grp), 1)
    perm = (r == jnp.where(c < LANES, 2 * c, 2 * (c - LANES) + 1)).astype(BF16)
    for g in range(w_ref.shape[3] // grp):
        w = w_ref[0, 0, :, g * grp:(g + 1) * grp].astype(BF16)
        o_ref[0, :, g * grp:(g + 1) * grp] = jnp.dot(w, perm, preferred_element_type=F32).astype(BF16)


def _wcast_kernel(w_ref, o_ref):
    o_ref[0] = w_ref[0, 0].astype(BF16)


def _wprep(w, layer, *, regroup, tr, tn):
    _, ne, r, c = w.shape
    return pl.pallas_call(
        _wprep_kernel if regroup else _wcast_kernel,
        grid=(ne, r // tr, c // tn),
        in_specs=[pl.BlockSpec((1, 1, tr, tn), lambda e, i, j: (layer, e, i, j))],
        out_specs=pl.BlockSpec((1, tr, tn), lambda e, i, j: (e, i, j)),
        out_shape=jax.ShapeDtypeStruct((ne, r, c), BF16),
        compiler_params=_cparams("arbitrary", "arbitrary", "arbitrary"),
        name="wprep" if regroup else "wcast",
    )(w)


def _regroup_gate_up(b):
    lead = b.shape[:-1]
    return b.reshape(lead + (-1, LANES, 2)).swapaxes(-1, -2).reshape(lead + (-1,))


def _moe_kernel(sbe_ref, live_ref, rows_hbm, h_hbm, wgu_ref, bgu_ref, wd_ref, bd_ref, y_hbm,
                rows_smem, xg, acc, sem_idx, sem_in, sem_out, *, nj):
    s = pl.program_id(0)
    j = pl.program_id(1)
    n_sb = pl.num_programs(0)
    nlive = live_ref[s]
    nsub = (nlive + MOE_SUB - 1) // MOE_SUB
    nsub_prev = jnp.where(s > 0, (live_ref[jnp.maximum(s - 1, 0)] + MOE_SUB - 1) // MOE_SUB, 0)
    par = s % 2
    grp_rows = MOE_TM // (MOE_SUBS_PER_BLOCK * nj)
    plan_words = 2 * MOE_TM

    def sub_rows(q, base=0):
        return pl.ds(pl.multiple_of(base + q * MOE_SUB, MOE_SUB), MOE_SUB)

    def start_rows(start_row, lo, hi):
        ngrp = (hi - lo) // MOE_ISSUE_UNROLL

        def group(g, c):
            for u in range(MOE_ISSUE_UNROLL):
                start_row(lo + g * MOE_ISSUE_UNROLL + u)
            return c

        def single(row, c):
            start_row(row)
            return c

        lax.fori_loop(0, ngrp, group, 0)
        lax.fori_loop(lo + ngrp * MOE_ISSUE_UNROLL, hi, single, 0)

    def wait_blocks(src, dst, sem, nblocks):
        def block(q, c):
            pltpu.make_async_copy(src.at[pl.ds(0, MOE_SUB), :], dst.at[pl.ds(0, MOE_SUB), :], sem).wait()
            return c

        lax.fori_loop(0, nblocks, block, 0)

    def idx_copy(sb, slot_of=None):
        slot = ((sb if slot_of is None else slot_of) % 4) * plan_words
        return pltpu.make_async_copy(rows_hbm.at[pl.ds(pl.multiple_of(sb * plan_words, plan_words), plan_words)],
                                     rows_smem.at[pl.ds(pl.multiple_of(slot, plan_words), plan_words)], sem_idx)

    def gather_row(sb, row):
        tok = rows_smem[(sb % 4) * plan_words + row]
        half = (sb % 2) * MOE_TM
        pltpu.make_async_copy(h_hbm.at[pl.ds(tok, 1), :], xg.at[pl.ds(half + row, 1), :], sem_in.at[sb % 2]).start()

    def scatter_prev_row(row):
        dst = rows_smem[((s + 3) % 4) * plan_words + MOE_TM + row]
        half = (1 - par) * MOE_TM
        pltpu.make_async_copy(acc.at[pl.ds(half + row, 1), :], y_hbm.at[pl.ds(dst, 1), :], sem_out.at[1 - par]).start()

    def remaining_groups(start_row, nsub_done):
        def per_j(jj, c):
            def per_q(q, c2):
                lo = (jj * MOE_SUBS_PER_BLOCK + q) * grp_rows
                start_rows(start_row, lo, lo + grp_rows)
                return c2

            lax.fori_loop(nsub_done, MOE_SUBS_PER_BLOCK, per_q, 0)
            return c

        lax.fori_loop(0, nj, per_j, 0)

    @pl.when(j == 0)
    def _prologue():
        @pl.when(s == 0)
        def _():
            for sb in range(2):
                idx_copy(sb).start()
                idx_copy(sb).wait()
            idx_copy(n_sb - 1, slot_of=3).start()
            idx_copy(n_sb - 1, slot_of=3).wait()
            other = pl.ds(MOE_TM, MOE_TM)
            acc[other, :] = jnp.zeros((MOE_TM, acc.shape[1]), F32)
            fill = pltpu.make_async_copy(acc.at[other, :], y_hbm.at[pl.ds(y_hbm.shape[0] - MOE_TM, MOE_TM), :],
                                         sem_out.at[0])
            fill.start()
            fill.wait()

        @pl.when(jnp.logical_and(s > 0, s + 1 < n_sb))
        def _():
            idx_copy(s + 1).wait()

        @pl.when(s + 2 < n_sb)
        def _():
            idx_copy(s + 2).start()

        @pl.when(nsub > 0)
        def _():
            remaining_groups(functools.partial(gather_row, s), nsub_prev)

        started = jnp.where(nsub > 0, MOE_SUBS_PER_BLOCK, nsub_prev)
        wait_blocks(h_hbm, xg, sem_in.at[par], started)

        def init(q, c):
            acc[sub_rows(q, par * MOE_TM), :] = jnp.broadcast_to(bd_ref[0], (MOE_SUB, acc.shape[1]))
            return c

        lax.fori_loop(0, MOE_SUBS_PER_BLOCK, init, 0)

    def compute(q, c):
        lo = (j * MOE_SUBS_PER_BLOCK + q) * grp_rows
        for u in range(grp_rows):
            gather_row(s + 1, lo + u)
            scatter_prev_row(lo + u)
        x = xg[sub_rows(q, par * MOE_TM), :].astype(BF16)
        hgu = jnp.dot(x, wgu_ref[0], preferred_element_type=F32) + bgu_ref[0]
        nchunk = hgu.shape[1] // LANES
        gate = jnp.concatenate([hgu[:, b * LANES:(b + 1) * LANES] for b in range(0, nchunk, 2)], axis=-1)
        up = jnp.concatenate([hgu[:, b * LANES:(b + 1) * LANES] for b in range(1, nchunk, 2)], axis=-1)
        gate = jnp.minimum(gate, SWIGLU_LIMIT)
        up = jnp.clip(up, -SWIGLU_LIMIT, SWIGLU_LIMIT)
        act = (up + 1.0) * (gate * jax.nn.sigmoid(SWIGLU_ALPHA * gate))
        acc[sub_rows(q, par * MOE_TM), :] += jnp.dot(act.astype(BF16), wd_ref[0], preferred_element_type=F32)
        return c

    lax.fori_loop(0, nsub, compute, 0)

    @pl.when(j == nj - 1)
    def _epilogue():
        @pl.when(nsub_prev > 0)
        def _():
            remaining_groups(scatter_prev_row, nsub)

        started = jnp.where(nsub_prev > 0, MOE_SUBS_PER_BLOCK, nsub)
        wait_blocks(acc, y_hbm, sem_out.at[1 - par], started)


def _moe(h2, sb_e, live, rows, wgu, bgu, wdn, bdn):
    n, d = h2.shape
    ne, f = wdn.shape[0], wdn.shape[1]
    nj = f // MOE_TF
    n_sb = sb_e.shape[0]

    def ff(s, j, live_ref):
        return jnp.where(live_ref[s] > 0, j, nj - 1)

    grid_spec = pltpu.PrefetchScalarGridSpec(
        num_scalar_prefetch=2,
        grid=(n_sb, nj),
        in_specs=[pl.BlockSpec(memory_space=pl.ANY),
                  pl.BlockSpec(memory_space=pl.ANY),
                  pl.BlockSpec((1, d, 2 * MOE_TF), lambda s, j, e, ns: (e[s], 0, ff(s, j, ns))),
                  pl.BlockSpec((1, 1, 2 * MOE_TF), lambda s, j, e, ns: (e[s], 0, ff(s, j, ns))),
                  pl.BlockSpec((1, MOE_TF, d), lambda s, j, e, ns: (e[s], ff(s, j, ns), 0)),
                  pl.BlockSpec((1, 1, d), lambda s, j, e, ns: (e[s], 0, 0))],
        out_specs=pl.BlockSpec(memory_space=pl.ANY),
        scratch_shapes=[pltpu.SMEM((4 * 2 * MOE_TM,), jnp.int32),
                        pltpu.VMEM((2 * MOE_TM, d), F32),
                        pltpu.VMEM((2 * MOE_TM, d), F32),
                        pltpu.SemaphoreType.DMA(()),
                        pltpu.SemaphoreType.DMA((2,)),
                        pltpu.SemaphoreType.DMA((2,))])
    assert MOE_TM % (MOE_SUBS_PER_BLOCK * nj) == 0 and n_sb >= 3
    return pl.pallas_call(
        functools.partial(_moe_kernel, nj=nj),
        grid_spec=grid_spec,
        out_shape=jax.ShapeDtypeStruct((TOP_K * n + MOE_TM, d), F32),
        compiler_params=_cparams("arbitrary", "arbitrary"),
        name="moe",
    )(sb_e, live, rows, h2, wgu, bgu, wdn, bdn)


def _combine_kernel(x_ref, y0_ref, y1_ref, y2_ref, y3_ref, gate_ref, gfin_ref, *o_refs, final, starts):
    g = gate_ref[...]
    x = x_ref[...]
    for k, y_ref in enumerate((y0_ref, y1_ref, y2_ref, y3_ref)):
        x = x + g[:, k:k + 1] * y_ref[...]
    x = _rms(x, gfin_ref[...]) if final else x
    if len(o_refs) == 1:
        o_refs[0][...] = x
        return
    i = pl.program_id(0)
    bounds = list(starts) + [pl.num_programs(0)]
    for k, o_ref in enumerate(o_refs):
        @pl.when(jnp.logical_and(i >= bounds[k], i < bounds[k + 1]))
        def _(o_ref=o_ref):
            o_ref[...] = x


def _combine(x1, y, gates, gfin, *, final, tm, split=None):
    n, d = x1.shape
    nb = n // tm
    y_spec = lambda k: pl.BlockSpec((tm, d), lambda i: (k * nb + i, 0))
    if split is None:
        starts = (0,)
        out_specs = pl.BlockSpec((tm, d), lambda i: (i, 0))
        out_shape = jax.ShapeDtypeStruct((n, d), F32)
    else:
        starts, out_specs, out_shape, s0 = [], [], [], 0
        for nk in split:
            assert nk % tm == 0
            out_specs.append(pl.BlockSpec((tm, d), lambda i, s0=s0, nbk=nk // tm: (jnp.clip(i - s0, 0, nbk - 1), 0)))
            out_shape.append(jax.ShapeDtypeStruct((nk, d), F32))
            starts.append(s0)
            s0 += nk // tm
        starts = tuple(starts)
    return pl.pallas_call(
        functools.partial(_combine_kernel, final=final, starts=starts),
        grid=(nb,),
        in_specs=[pl.BlockSpec((tm, d), lambda i: (i, 0)), y_spec(0), y_spec(1), y_spec(2), y_spec(3),
                  pl.BlockSpec((tm, TOP_K), lambda i: (i, 0)),
                  pl.BlockSpec((1, d), lambda i: (0, 0))],
        out_specs=out_specs,
        out_shape=out_shape,
        compiler_params=_cparams("arbitrary"),
        name="combine",
    )(x1, y, y, y, y, gates, gfin.reshape(1, d))


def _pick(n, *cands):
    for c in cands:
        if n % c == 0:
            return c
    raise ValueError(f"no tile for {n}")


def kernel(x_prompt, x_sample, g_mix, w_in, rpb, conv_w, conv_b, w_rg_a, b_rg_a, w_rg_x, b_rg_x, rg_lambda,
           g_attn_out, g_lru_out, w_out, g_ffn, w_router, b_router, w_gate_up, b_gate_up, w_down, b_down, g_final):
    depth = w_in.shape[0]
    d = x_prompt.shape[-1]
    attn_w = rpb.shape[1] * HEAD_DIM
    lru_w = conv_w.shape[-1]
    ne = w_router.shape[-1]
    f = w_down.shape[2]
    seqs = []
    off = 0
    for xs in (x_prompt, x_sample):
        seqs.append((off, xs.shape[0], xs.shape[1]))
        off += xs.shape[0] * xs.shape[1]
    n = off
    x = jnp.concatenate([x_prompt.reshape(-1, d), x_sample.reshape(-1, d)], axis=0)
    assert attn_w % LANES == 0 and lru_w == attn_w, "column blocks of z assume equal attention / recurrent widths"

    for l in range(depth):
        z = _inproj(x, g_mix[l], w_in[l].astype(BF16), tm=_pick(n, 1024, 512, 256), tn=_pick(w_in.shape[2], 1024, 512, 256))
        bias = _na_bias_table(rpb[l])
        wa, wx = w_rg_a[l].astype(BF16), w_rg_x[l].astype(BF16)
        attn, h = [], []
        for (o, b, t) in seqs:
            attn.append(_na(z, bias, row_off=o, batch=b, t=t, attn_w=attn_w, tq=_pick(t, 512, 256, 128, 64)))
            lru_args = (z, conv_w[l], conv_b[l], wa, b_rg_a[l], wx, b_rg_x[l], rg_lambda[l])
            if b == SUBLANES and n % t == 0 and o % (b * t) == 0:
                h.append(_lru_batched(*lru_args, row_off=o, t=t, col_blk=3, tt=_pick(t, 128, 64)))
            else:
                h.append(_lru(*lru_args, row_off=o, batch=b, t=t, col_blk=3, tt=_pick(t, 512, 256, 128, 64)))
        x1, h2, top_idx, gates = _outproj(attn, h, z, x, g_attn_out[l], g_lru_out[l], w_out[l].astype(BF16),
                                          g_ffn[l], w_router[l], b_router[l], g_blk=4, tm=_pick(n, 256, 128))
        sb_e, live, rows = _moe_plan(top_idx, ne)
        wgu = _wprep(w_gate_up, l, regroup=True, tr=d, tn=_pick(2 * f, 1024, 512, 256))
        wdn = _wprep(w_down, l, regroup=False, tr=_pick(f, 512, 256), tn=d)
        bgu = _regroup_gate_up(b_gate_up[l]).reshape(ne, 1, 2 * f)
        y = _moe(h2, sb_e, live, rows, wgu, bgu, wdn, b_down[l].reshape(ne, 1, d))
        last = l == depth - 1
        x = _combine(x1, y, gates, g_final, final=last, tm=_pick(n, 256, 128),
                     split=tuple(b * t for (_, b, t) in seqs) if last else None)

    return x[0].reshape(x_prompt.shape), x[1].reshape(x_sample.shape)
```

```python
import functools

import jax
import jax.numpy as jnp
from jax import lax
from jax.experimental import pallas as pl
from jax.experimental.pallas import tpu as pltpu

F32 = jnp.float32
BF16 = jnp.bfloat16

GRID_W = 64
HEAD_DIM = 64
NA_ROWS = 8
NA_COLS = 16
CONV_W = 4
RG_C = 8.0
LRU_BLOCK_W = 128
TOP_K = 4
SWIGLU_LIMIT = 7.0
SWIGLU_ALPHA = 1.702
NORM_EPS = 1e-5
MASK_BIAS = -1e30

LANES = 128
SUBLANES = 8
BF16_SUBLANES = 16
VMEM_LIMIT = 56 * 1024 * 1024

MOE_SUB = 256
MOE_SUBS_PER_BLOCK = 4
MOE_TM = MOE_SUB * MOE_SUBS_PER_BLOCK
MOE_TF = 512
MOE_ISSUE_UNROLL = 4


def _cparams(*sem):
    return pltpu.CompilerParams(dimension_semantics=sem, vmem_limit_bytes=VMEM_LIMIT)


def _rms(x, g):
    return x * lax.rsqrt(jnp.mean(x * x, axis=-1, keepdims=True) + NORM_EPS) * g


def _inproj_kernel(x_ref, g_ref, w_ref, o_ref, xn_ref):
    @pl.when(pl.program_id(1) == 0)
    def _():
        xn_ref[...] = _rms(x_ref[...], g_ref[...]).astype(BF16)

    o_ref[...] = jnp.dot(xn_ref[...], w_ref[...], preferred_element_type=F32).astype(o_ref.dtype)


def _inproj(x, g, w, *, tm, tn):
    n, d = x.shape
    wn = w.shape[1]
    return pl.pallas_call(
        _inproj_kernel,
        grid=(n // tm, wn // tn),
        in_specs=[pl.BlockSpec((tm, d), lambda i, j: (i, 0)),
                  pl.BlockSpec((1, d), lambda i, j: (0, 0)),
                  pl.BlockSpec((d, tn), lambda i, j: (0, j))],
        out_specs=pl.BlockSpec((tm, tn), lambda i, j: (i, j)),
        out_shape=jax.ShapeDtypeStruct((n, wn), BF16),
        scratch_shapes=[pltpu.VMEM((tm, d), BF16)],
        compiler_params=_cparams("arbitrary", "arbitrary"),
        name="inproj",
    )(x, g.reshape(1, d), w)


def _na_bias_table(rpb):
    h = rpb.shape[0]
    c = jnp.arange(GRID_W)
    col_start = jnp.clip(c - NA_COLS // 2, 0, GRID_W - NA_COLS)
    j = jnp.arange(GRID_W)
    inwin = (j[None, :] >= col_start[:, None]) & (j[None, :] < col_start[:, None] + NA_COLS)
    dc = jnp.clip(j[None, :] - c[:, None] + (NA_COLS - 1), 0, 2 * NA_COLS - 2)
    delta = jnp.arange(NA_ROWS)
    kk = jnp.arange(NA_ROWS)
    dr = kk[None, :] - delta[:, None] + (NA_ROWS - 1)
    row_sel = jax.nn.one_hot(dr, 2 * NA_ROWS - 1, dtype=F32)
    col_sel = jax.nn.one_hot(dc, 2 * NA_COLS - 1, dtype=F32)
    t = jnp.einsum("dka,hab,cjb->hdkcj", row_sel, rpb.astype(F32), col_sel, precision=lax.Precision.HIGHEST)
    t = jnp.where(inwin[None, None, None], t, MASK_BIAS)
    t = t.transpose(0, 1, 3, 2, 4).reshape(h // 2, 2, NA_ROWS, GRID_W, NA_ROWS * GRID_W)
    return t.transpose(0, 2, 1, 3, 4).reshape(h // 2, NA_ROWS, 2 * GRID_W, NA_ROWS * GRID_W)


def _na_kernel(q_ref, k_ref, v_ref, b_ref, o_ref, *, rows, rows_per_tile):
    ti = pl.program_id(2)
    lane = lax.broadcasted_iota(jnp.int32, (GRID_W, 2 * HEAD_DIM), 1)
    first_head = lane < HEAD_DIM
    win = NA_ROWS * GRID_W

    def body(rr, carry):
        r = ti * rows_per_tile + rr
        r0 = jnp.clip(r - NA_ROWS // 2, 0, rows - NA_ROWS)
        q = q_ref[pl.ds(pl.multiple_of(rr * GRID_W, GRID_W), GRID_W), :] * (HEAD_DIM ** -0.5)
        zero = jnp.zeros_like(q)
        qm = jnp.concatenate([jnp.where(first_head, q, zero), jnp.where(first_head, zero, q)], axis=0)
        ks = pl.multiple_of(r0 * GRID_W, GRID_W)
        kw = k_ref[pl.ds(ks, win), :]
        vw = v_ref[pl.ds(ks, win), :]
        s = lax.dot_general(qm, kw, (((1,), (1,)), ((), ())), preferred_element_type=F32)
        s = s + b_ref[0, r - r0]
        m = jnp.max(s, axis=-1, keepdims=True)
        p = jnp.exp(s - m)
        l = jnp.sum(p, axis=-1, keepdims=True)
        o = jnp.dot(p.astype(BF16), vw, preferred_element_type=F32) / l
        out = jnp.where(first_head, o[:GRID_W], o[GRID_W:])
        o_ref[pl.ds(pl.multiple_of(rr * GRID_W, GRID_W), GRID_W), :] = out.astype(o_ref.dtype)
        return carry

    lax.fori_loop(0, rows_per_tile, body, 0, unroll=True)


def _na(z, bias, *, row_off, batch, t, attn_w, tq):
    rows = t // GRID_W
    assert rows >= NA_ROWS and t % tq == 0 and row_off % t == 0 and tq % GRID_W == 0
    hp = attn_w // LANES
    nq = t // tq
    qoff, koff = row_off // tq, row_off // t
    kern = functools.partial(_na_kernel, rows=rows, rows_per_tile=tq // GRID_W)
    return pl.pallas_call(
        kern,
        grid=(batch, hp, nq),
        in_specs=[pl.BlockSpec((tq, LANES), lambda b, h, i: (qoff + b * nq + i, h)),
                  pl.BlockSpec((t, LANES), lambda b, h, i: (koff + b, hp + h)),
                  pl.BlockSpec((t, LANES), lambda b, h, i: (koff + b, 2 * hp + h)),
                  pl.BlockSpec((1,) + bias.shape[1:], lambda b, h, i: (h, 0, 0, 0))],
        out_specs=pl.BlockSpec((tq, LANES), lambda b, h, i: (b * nq + i, h)),
        out_shape=jax.ShapeDtypeStruct((batch * t, attn_w), BF16),
        compiler_params=_cparams("arbitrary", "arbitrary", "arbitrary"),
        name="na",
    )(z, z, z, bias)


def _softplus(x):
    return jnp.maximum(x, 0.0) + jnp.log(1.0 + jnp.exp(-jnp.abs(x)))


def _sigmoid(x):
    return 0.5 * jnp.tanh(0.5 * x) + 0.5


def _lru_coeffs(u, prev, nxt, ext, refs, *, d, ci, tt, nt, t, store):
    cw_ref, cb_ref, wa_ref, ba_ref, wx_ref, bx_ref, lam_ref = refs
    halo = SUBLANES
    ext[pl.ds(0, halo), :] = jnp.where(ci == 0, 0.0, prev)
    ext[pl.ds(halo, tt), :] = u
    ext[pl.ds(halo + tt, halo), :] = jnp.where(ci == nt - 1, 0.0, nxt)
    c = cb_ref[...]
    for j in range(CONV_W):
        c = c + cw_ref[pl.ds(j, 1), :] * ext[pl.ds(halo - CONV_W // 2 + j, tt), :]
    cbf = c.astype(BF16)

    tpos = ci * tt + lax.broadcasted_iota(jnp.int32, (tt, 1), 0)
    is_first = tpos == jnp.where(d == 0, 0, t - 1)
    sp = _softplus(-lam_ref[0])
    for n in range(wa_ref.shape[1]):
        sl = slice(n * LRU_BLOCK_W, (n + 1) * LRU_BLOCK_W)
        cn = cbf[:, sl]
        r = _sigmoid(jnp.dot(cn, wa_ref[0, n], preferred_element_type=F32) + ba_ref[0, :, sl])
        ig = _sigmoid(jnp.dot(cn, wx_ref[0, n], preferred_element_type=F32) + bx_ref[0, :, sl])
        a = jnp.exp(-RG_C * r * sp[:, sl])
        mult = jnp.where(is_first, 1.0, jnp.sqrt(1.0 - a * a))
        store(n, a, mult * (ig * c[:, sl]))


def _lru_batched_kernel(u_ref, up_ref, un_ref, cw_ref, cb_ref, wa_ref, ba_ref, wx_ref, bx_ref, lam_ref, o_ref,
                        ext, a_scr, b_scr, carry, *, tt, nt, t):
    d = pl.program_id(0)
    i = pl.program_id(1)
    ci = jnp.where(d == 0, i, nt - 1 - i)
    nblk = a_scr.shape[0]
    refs = (cw_ref, cb_ref, wa_ref, ba_ref, wx_ref, bx_ref, lam_ref)

    def per_sequence(b, c0):
        def store(n, a, bb):
            a_scr[n, :, b, :] = a
            b_scr[n, :, b, :] = bb

        _lru_coeffs(u_ref[b].astype(F32), up_ref[b].astype(F32)[BF16_SUBLANES - SUBLANES:, :],
                    un_ref[b].astype(F32)[:SUBLANES, :], ext, refs, d=d, ci=ci, tt=tt, nt=nt, t=t, store=store)
        return c0

    lax.fori_loop(0, SUBLANES, per_sequence, 0)

    @pl.when(i == 0)
    def _():
        carry[...] = jnp.zeros_like(carry)

    def step(jj, hs):
        j = jnp.where(d == 0, jj, tt - 1 - jj)
        new = []
        for n in range(nblk):
            h = a_scr[n, j] * hs[n] + b_scr[n, j]
            b_scr[n, j] = h
            new.append(h)
        return tuple(new)

    hs = lax.fori_loop(0, tt, step, tuple(carry[n] for n in range(nblk)))
    for n in range(nblk):
        carry[n] = hs[n]

    def write(b, c0):
        for n in range(nblk):
            o_ref[0, n, b] = b_scr[n, :, b, :]
        return c0

    lax.fori_loop(0, SUBLANES, write, 0)


def _lru_batched(z, cw, cb, wa, ba, wx, bx, lam, *, row_off, t, col_blk, tt):
    c = cw.shape[1]
    nblk = c // LRU_BLOCK_W
    nt = t // tt
    ntot = z.shape[0]
    nb = SUBLANES
    assert t % tt == 0 and ntot % t == 0 and row_off % (nb * t) == 0 and tt % BF16_SUBLANES == 0
    z3 = z.reshape(ntot // t, t, z.shape[1])
    g = row_off // (nb * t)
    hb = BF16_SUBLANES

    def chunk(dd, i):
        return jnp.where(dd == 0, i, nt - 1 - i)

    kern = functools.partial(_lru_batched_kernel, tt=tt, nt=nt, t=t)
    dir_w = lambda dd, i: (dd, 0, 0, 0)
    dir_v = lambda dd, i: (dd, 0, 0)
    out = pl.pallas_call(
        kern,
        grid=(2, nt),
        in_specs=[pl.BlockSpec((nb, tt, c), lambda dd, i: (g, chunk(dd, i), col_blk)),
                  pl.BlockSpec((nb, hb, c), lambda dd, i: (g, jnp.maximum(chunk(dd, i) * (tt // hb) - 1, 0), col_blk)),
                  pl.BlockSpec((nb, hb, c), lambda dd, i: (g, jnp.minimum((chunk(dd, i) + 1) * (tt // hb), t // hb - 1), col_blk)),
                  pl.BlockSpec((CONV_W, c), lambda dd, i: (0, 0)),
                  pl.BlockSpec((1, c), lambda dd, i: (0, 0)),
                  pl.BlockSpec((1, nblk, LRU_BLOCK_W, LRU_BLOCK_W), dir_w),
                  pl.BlockSpec((1, 1, c), dir_v),
                  pl.BlockSpec((1, nblk, LRU_BLOCK_W, LRU_BLOCK_W), dir_w),
                  pl.BlockSpec((1, 1, c), dir_v),
                  pl.BlockSpec((1, 1, c), dir_v)],
        out_specs=pl.BlockSpec((1, nblk, nb, tt, LRU_BLOCK_W), lambda dd, i: (dd, 0, 0, chunk(dd, i), 0)),
        out_shape=jax.ShapeDtypeStruct((2, nblk, nb, t, LRU_BLOCK_W), F32),
        scratch_shapes=[pltpu.VMEM((tt + 2 * SUBLANES, c), F32),
                        pltpu.VMEM((nblk, tt, nb, LRU_BLOCK_W), F32),
                        pltpu.VMEM((nblk, tt, nb, LRU_BLOCK_W), F32),
                        pltpu.VMEM((nblk, nb, LRU_BLOCK_W), F32)],
        compiler_params=_cparams("arbitrary", "arbitrary"),
        name="lru_batched",
    )(z3, z3, z3, cw, cb.reshape(1, c), wa, ba.reshape(2, 1, c), wx, bx.reshape(2, 1, c), lam.reshape(2, 1, c))
    return out.reshape(2, nblk, nb * t, LRU_BLOCK_W)


def _lru_kernel(u_ref, up_ref, un_ref, cw_ref, cb_ref, wa_ref, ba_ref, wx_ref, bx_ref, lam_ref, o_ref,
                ext, a_scr, b_scr, carry, *, tt, nt, t):
    d = pl.program_id(1)
    i = pl.program_id(2)
    ci = jnp.where(d == 0, i, nt - 1 - i)
    nblk = a_scr.shape[0]
    seg = tt // SUBLANES

    def store(n, a, bb):
        a_scr[n] = a
        b_scr[n] = bb

    _lru_coeffs(u_ref[...].astype(F32), up_ref[...].astype(F32)[BF16_SUBLANES - SUBLANES:, :],
                un_ref[...].astype(F32)[:SUBLANES, :], ext,
                (cw_ref, cb_ref, wa_ref, ba_ref, wx_ref, bx_ref, lam_ref), d=d, ci=ci, tt=tt, nt=nt, t=t, store=store)

    @pl.when(i == 0)
    def _():
        carry[...] = jnp.zeros_like(carry)

    def step_index(jj):
        return jnp.where(d == 0, jj, seg - 1 - jj)

    def load(ref, n, j):
        return ref[n, pl.ds(j, SUBLANES, stride=seg), :]

    def local_step(jj, hp):
        j = step_index(jj)
        hs, ps = hp
        new_h, new_p = [], []
        for n in range(nblk):
            a = load(a_scr, n, j)
            new_h.append(a * hs[n] + load(b_scr, n, j))
            new_p.append(a * ps[n])
        return tuple(new_h), tuple(new_p)

    zeros = tuple(jnp.zeros((SUBLANES, LRU_BLOCK_W), F32) for _ in range(nblk))
    ones = tuple(jnp.ones((SUBLANES, LRU_BLOCK_W), F32) for _ in range(nblk))
    h_end, p_end = lax.fori_loop(0, seg, local_step, (zeros, ones))

    cin, cout = [], []
    for n in range(nblk):
        c0 = carry[pl.ds(n, 1), :]
        fwd, cur = [], c0
        for s in range(SUBLANES):
            fwd.append(cur)
            cur = h_end[n][s:s + 1] + p_end[n][s:s + 1] * cur
        fwd_out = cur
        bwd, cur = [None] * SUBLANES, c0
        for s in reversed(range(SUBLANES)):
            bwd[s] = cur
            cur = h_end[n][s:s + 1] + p_end[n][s:s + 1] * cur
        bwd_out = cur
        cin.append(jnp.where(d == 0, jnp.concatenate(fwd, axis=0), jnp.concatenate(bwd, axis=0)))
        cout.append(jnp.where(d == 0, fwd_out, bwd_out))

    def final_step(jj, hs):
        j = step_index(jj)
        new_h = []
        for n in range(nblk):
            h = load(a_scr, n, j) * hs[n] + load(b_scr, n, j)
            o_ref[0, n, pl.ds(j, SUBLANES, stride=seg), :] = h
            new_h.append(h)
        return tuple(new_h)

    lax.fori_loop(0, seg, final_step, tuple(cin))
    for n in range(nblk):
        carry[pl.ds(n, 1), :] = cout[n]


def _lru(z, cw, cb, wa, ba, wx, bx, lam, *, row_off, batch, t, col_blk, tt):
    c = cw.shape[1]
    nblk = c // LRU_BLOCK_W
    nt = t // tt
    ntot = z.shape[0]
    assert t % tt == 0 and row_off % tt == 0 and tt % (SUBLANES * SUBLANES) == 0
    hb = BF16_SUBLANES
    base = row_off // tt

    def chunk(dd, i):
        return jnp.where(dd == 0, i, nt - 1 - i)

    def u_map(b, dd, i):
        return (base + b * nt + chunk(dd, i), col_blk)

    def prev_map(b, dd, i):
        return (jnp.maximum((row_off + b * t + chunk(dd, i) * tt) // hb - 1, 0), col_blk)

    def next_map(b, dd, i):
        return (jnp.minimum((row_off + b * t + (chunk(dd, i) + 1) * tt) // hb, ntot // hb - 1), col_blk)

    kern = functools.partial(_lru_kernel, tt=tt, nt=nt, t=t)
    dir_w = lambda b, dd, i: (dd, 0, 0, 0)
    dir_v = lambda b, dd, i: (dd, 0, 0)
    return pl.pallas_call(
        kern,
        grid=(batch, 2, nt),
        in_specs=[pl.BlockSpec((tt, c), u_map),
                  pl.BlockSpec((hb, c), prev_map),
                  pl.BlockSpec((hb, c), next_map),
                  pl.BlockSpec((CONV_W, c), lambda b, dd, i: (0, 0)),
                  pl.BlockSpec((1, c), lambda b, dd, i: (0, 0)),
                  pl.BlockSpec((1, nblk, LRU_BLOCK_W, LRU_BLOCK_W), dir_w),
                  pl.BlockSpec((1, 1, c), dir_v),
                  pl.BlockSpec((1, nblk, LRU_BLOCK_W, LRU_BLOCK_W), dir_w),
                  pl.BlockSpec((1, 1, c), dir_v),
                  pl.BlockSpec((1, 1, c), dir_v)],
        out_specs=pl.BlockSpec((1, nblk, tt, LRU_BLOCK_W), lambda b, dd, i: (dd, 0, b * nt + chunk(dd, i), 0)),
        out_shape=jax.ShapeDtypeStruct((2, nblk, batch * t, LRU_BLOCK_W), F32),
        scratch_shapes=[pltpu.VMEM((tt + 2 * SUBLANES, c), F32),
                        pltpu.VMEM((nblk, tt, LRU_BLOCK_W), F32),
                        pltpu.VMEM((nblk, tt, LRU_BLOCK_W), F32),
                        pltpu.VMEM((nblk, LRU_BLOCK_W), F32)],
        compiler_params=_cparams("arbitrary", "arbitrary", "arbitrary"),
        name="lru",
    )(z, z, z, cw, cb.reshape(1, c), wa, ba.reshape(2, 1, c), wx, bx.reshape(2, 1, c), lam.reshape(2, 1, c))


def _gelu_tanh(x):
    return 0.5 * x * (1.0 + jnp.tanh(0.7978845608028654 * (x + 0.044715 * (x * x * x))))


def _outproj_kernel(*refs, starts):
    ng = len(starts)
    groups = [refs[3 * k:3 * k + 3] for k in range(ng)]
    (g_ref, x_ref, ga_ref, gl_ref, wo_ref, gf_ref, wr_ref, br_ref,
     x1_ref, h2_ref, idx_ref, gate_ref) = refs[3 * ng:]
    i = pl.program_id(0)
    nblk = groups[0][1].shape[1]

    def read(k):
        attn_ref, hf_ref, hb_ref = groups[k]
        return (attn_ref[...].astype(F32),
                jnp.concatenate([hf_ref[0, n] + hb_ref[0, n] for n in range(nblk)], axis=-1))

    attn, h = read(0)
    for k in range(1, ng):
        attn_k, h_k = read(k)
        attn = jnp.where(i >= starts[k], attn_k, attn)
        h = jnp.where(i >= starts[k], h_k, h)
    an = _rms(attn, ga_ref[...])
    rn = _rms(h * _gelu_tanh(g_ref[...].astype(F32)), gl_ref[...])
    mixed = jnp.concatenate([an, rn], axis=-1).astype(BF16)
    x1 = x_ref[...] + jnp.dot(mixed, wo_ref[...], preferred_element_type=F32)
    x1_ref[...] = x1
    h2 = _rms(x1, gf_ref[...])
    h2_ref[...] = h2

    logits = jnp.dot(h2, wr_ref[...], preferred_element_type=F32, precision=lax.Precision.HIGHEST) + br_ref[...]
    ne = logits.shape[-1]
    eidx = lax.broadcasted_iota(jnp.int32, logits.shape, 1)
    kidx = lax.broadcasted_iota(jnp.int32, idx_ref.shape, 1)
    vals = jnp.zeros(gate_ref.shape, F32)
    idxs = jnp.zeros(idx_ref.shape, jnp.int32)
    cur = logits
    for k in range(TOP_K):
        m = jnp.max(cur, axis=-1, keepdims=True)
        sel = jnp.min(jnp.where(cur == m, eidx, ne), axis=-1, keepdims=True)
        vals = jnp.where(kidx == k, m, vals)
        idxs = jnp.where(kidx == k, sel, idxs)
        cur = jnp.where(eidx == sel, -jnp.inf, cur)
    e = jnp.exp(vals - vals[:, 0:1])
    gate_ref[...] = e / jnp.sum(e, axis=-1, keepdims=True)
    idx_ref[...] = idxs


def _outproj(attns, hs, z, x, ga, gl, wo, gf, wr, br, *, g_blk, tm):
    n, d = x.shape
    aw = attns[0].shape[1]
    nblk = hs[0].shape[1]
    c = nblk * LRU_BLOCK_W
    ne = wr.shape[1]
    const = lambda i: (0, 0)
    starts, group_specs, group_args = [], [], []
    s0 = 0
    for attn, h in zip(attns, hs):
        nb = attn.shape[0] // tm
        assert attn.shape[0] % tm == 0
        local = lambda i, s0=s0, nb=nb: jnp.clip(i - s0, 0, nb - 1)
        group_specs += [pl.BlockSpec((tm, aw), lambda i, f=local: (f(i), 0)),
                        pl.BlockSpec((1, nblk, tm, LRU_BLOCK_W), lambda i, f=local: (0, 0, f(i), 0)),
                        pl.BlockSpec((1, nblk, tm, LRU_BLOCK_W), lambda i, f=local: (1, 0, f(i), 0))]
        group_args += [attn, h, h]
        starts.append(s0)
        s0 += nb
    assert s0 == n // tm
    return pl.pallas_call(
        functools.partial(_outproj_kernel, starts=tuple(starts)),
        grid=(n // tm,),
        in_specs=group_specs + [
                  pl.BlockSpec((tm, c), lambda i: (i, g_blk)),
                  pl.BlockSpec((tm, d), lambda i: (i, 0)),
                  pl.BlockSpec((1, aw), const),
                  pl.BlockSpec((1, c), const),
                  pl.BlockSpec((d, d), const),
                  pl.BlockSpec((1, d), const),
                  pl.BlockSpec((d, ne), const),
                  pl.BlockSpec((1, ne), const)],
        out_specs=[pl.BlockSpec((tm, d), lambda i: (i, 0)),
                   pl.BlockSpec((tm, d), lambda i: (i, 0)),
                   pl.BlockSpec((tm, TOP_K), lambda i: (i, 0)),
                   pl.BlockSpec((tm, TOP_K), lambda i: (i, 0))],
        out_shape=[jax.ShapeDtypeStruct((n, d), F32),
                   jax.ShapeDtypeStruct((n, d), F32),
                   jax.ShapeDtypeStruct((n, TOP_K), jnp.int32),
                   jax.ShapeDtypeStruct((n, TOP_K), F32)],
        compiler_params=_cparams("arbitrary"),
        name="outproj",
    )(*group_args, z, x, ga.reshape(1, aw), gl.reshape(1, c), wo, gf.reshape(1, d), wr, br.reshape(1, ne))


def _moe_plan(top_idx, n_experts):
    n = top_idx.shape[0]
    m = n * TOP_K
    n_sb = -(-m // MOE_TM) + n_experts + 1
    flat_e = top_idx.reshape(-1)
    order = jnp.argsort(flat_e).astype(jnp.int32)
    counts = jnp.sum((flat_e[:, None] == jnp.arange(n_experts)[None, :]).astype(jnp.int32), axis=0)
    start = jnp.cumsum(counts) - counts
    sb_per_e = (counts + MOE_TM - 1) // MOE_TM
    sb_end = jnp.cumsum(sb_per_e)
    sb = jnp.arange(n_sb, dtype=jnp.int32)
    sb_e = jnp.minimum(jnp.sum((sb_end[None, :] <= sb[:, None]).astype(jnp.int32), axis=1), n_experts - 1)
    local = sb - (sb_end - sb_per_e)[sb_e]
    valid = jnp.clip(counts[sb_e] - local * MOE_TM, 0, MOE_TM)
    valid = jnp.where(sb < sb_end[-1], valid, 0).astype(jnp.int32)
    r = jnp.arange(MOE_TM, dtype=jnp.int32)
    live = r[None, :] < valid[:, None]
    sorted_pos = jnp.clip(start[sb_e][:, None] + local[:, None] * MOE_TM + r[None, :], 0, m - 1)
    a = order[sorted_pos]
    src = jnp.where(live, a // TOP_K, 0)
    dst = jnp.where(live, (a % TOP_K) * n + a // TOP_K, m + r[None, :])
    rows = jnp.concatenate([src, dst], axis=1).reshape(-1).astype(jnp.int32)
    return sb_e, valid, rows


def _wprep_kernel(w_ref, o_ref):
    grp = 2 * LANES
    r = lax.broadcasted_iota(jnp.int32, (grp, grp), 0)
    c = lax.broadcasted_iota(jnp.int32, (grp, grp), 1)
    perm = (r == jnp.where(c < LANES, 2 * c, 2 * (c - LANES) + 1)).astype(BF16)
    for g in range(w_ref.shape[3] // grp):
        w = w_ref[0, 0, :, g * grp:(g + 1) * grp].astype(BF16)
        o_ref[0, :, g * grp:(g + 1) * grp] = jnp.dot(w, perm, preferred_element_type=F32).astype(BF16)


def _wcast_kernel(w_ref, o_ref):
    o_ref[0] = w_ref[0, 0].astype(BF16)


def _wprep(w, layer, *, regroup, tr, tn):
    _, ne, r, c = w.shape
    return pl.pallas_call(
        _wprep_kernel if regroup else _wcast_kernel,
        grid=(ne, r // tr, c // tn),
        in_specs=[pl.BlockSpec((1, 1, tr, tn), lambda e, i, j: (layer, e, i, j))],
        out_specs=pl.BlockSpec((1, tr, tn), lambda e, i, j: (e, i, j)),
        out_shape=jax.ShapeDtypeStruct((ne, r, c), BF16),
        compiler_params=_cparams("arbitrary", "arbitrary", "arbitrary"),
        name="wprep" if regroup else "wcast",
    )(w)


def _regroup_gate_up(b):
    lead = b.shape[:-1]
    return b.reshape(lead + (-1, LANES, 2)).swapaxes(-1, -2).reshape(lead + (-1,))


def _moe_kernel(sbe_ref, live_ref, rows_hbm, h_hbm, wgu_ref, bgu_ref, wd_ref, bd_ref, y_hbm,
                rows_smem, xg0, xg1, acc0, acc1, sem_idx, sem_in, sem_out, *, nj):
    s = pl.program_id(0)
    j = pl.program_id(1)
    n_sb = pl.num_programs(0)
    xgs, accs = (xg0, xg1), (acc0, acc1)
    nsub = (live_ref[s] + MOE_SUB - 1) // MOE_SUB
    nsub_prev = jnp.where(s > 0, (live_ref[jnp.maximum(s - 1, 0)] + MOE_SUB - 1) // MOE_SUB, 0)
    grp_rows = MOE_TM // (MOE_SUBS_PER_BLOCK * nj)
    plan_words = 2 * MOE_TM

    def sub_rows(q, base=0):
        return pl.ds(pl.multiple_of(base + q * MOE_SUB, MOE_SUB), MOE_SUB)

    def start_rows(start_row, lo, hi):
        ngrp = (hi - lo) // MOE_ISSUE_UNROLL

        def group(g, c):
            for u in range(MOE_ISSUE_UNROLL):
                start_row(lo + g * MOE_ISSUE_UNROLL + u)
            return c

        def single(row, c):
            start_row(row)
            return c

        lax.fori_loop(0, ngrp, group, 0)
        lax.fori_loop(lo + ngrp * MOE_ISSUE_UNROLL, hi, single, 0)

    def wait_blocks(src, dst, sem, nblocks):
        def block(q, c):
            pltpu.make_async_copy(src.at[pl.ds(0, MOE_SUB), :], dst.at[pl.ds(0, MOE_SUB), :], sem).wait()
            return c

        lax.fori_loop(0, nblocks, block, 0)

    def idx_copy(sb, slot_of=None):
        slot = ((sb if slot_of is None else slot_of) % 4) * plan_words
        return pltpu.make_async_copy(rows_hbm.at[pl.ds(pl.multiple_of(sb * plan_words, plan_words), plan_words)],
                                     rows_smem.at[pl.ds(pl.multiple_of(slot, plan_words), plan_words)], sem_idx)

    def gather_row(sb, buf, sem, row):
        tok = rows_smem[(sb % 4) * plan_words + row]
        pltpu.make_async_copy(h_hbm.at[pl.ds(tok, 1), :], buf.at[pl.ds(row, 1), :], sem).start()

    def scatter_prev_row(buf, sem, row):
        dst = rows_smem[((s + 3) % 4) * plan_words + MOE_TM + row]
        pltpu.make_async_copy(buf.at[pl.ds(row, 1), :], y_hbm.at[pl.ds(dst, 1), :], sem).start()

    def remaining_groups(start_row, nsub_done):
        def per_j(jj, c):
            def per_q(q, c2):
                lo = (jj * MOE_SUBS_PER_BLOCK + q) * grp_rows
                start_rows(start_row, lo, lo + grp_rows)
                return c2

            lax.fori_loop(nsub_done, MOE_SUBS_PER_BLOCK, per_q, 0)
            return c

        lax.fori_loop(0, nj, per_j, 0)

    def run(p):
        xg_cur, xg_nxt = xgs[p], xgs[1 - p]
        acc_cur, acc_prv = accs[p], accs[1 - p]
        gather_cur = functools.partial(gather_row, s, xg_cur, sem_in.at[p])
        gather_nxt = functools.partial(gather_row, s + 1, xg_nxt, sem_in.at[1 - p])
        scatter_prv = functools.partial(scatter_prev_row, acc_prv, sem_out.at[1 - p])

        @pl.when(j == 0)
        def _prologue():
            if p == 0:
                @pl.when(s == 0)
                def _():
                    for sb in range(2):
                        idx_copy(sb).start()
                        idx_copy(sb).wait()
                    idx_copy(n_sb - 1, slot_of=3).start()
                    idx_copy(n_sb - 1, slot_of=3).wait()
                    acc_prv[...] = jnp.zeros_like(acc_prv)
                    fill = pltpu.make_async_copy(acc_prv, y_hbm.at[pl.ds(y_hbm.shape[0] - MOE_TM, MOE_TM), :],
                                                 sem_out.at[0])
                    fill.start()
                    fill.wait()

            @pl.when(jnp.logical_and(s > 0, s + 1 < n_sb))
            def _():
                idx_copy(s + 1).wait()

            @pl.when(s + 2 < n_sb)
            def _():
                idx_copy(s + 2).start()

            @pl.when(nsub > 0)
            def _():
                remaining_groups(gather_cur, nsub_prev)

            started = jnp.where(nsub > 0, MOE_SUBS_PER_BLOCK, nsub_prev)
            wait_blocks(h_hbm, xg_cur, sem_in.at[p], started)

            def init(q, c):
                acc_cur[sub_rows(q), :] = jnp.broadcast_to(bd_ref[0], (MOE_SUB, acc_cur.shape[1]))
                return c

            lax.fori_loop(0, MOE_SUBS_PER_BLOCK, init, 0)

        def compute(q, c):
            lo = pl.multiple_of((j * MOE_SUBS_PER_BLOCK + q) * grp_rows, grp_rows)
            for u in range(grp_rows):
                gather_nxt(lo + u)
                scatter_prv(lo + u)
            x = xg_cur[sub_rows(q), :].astype(BF16)
            hgu = jnp.dot(x, wgu_ref[0], preferred_element_type=F32) + bgu_ref[0]
            nchunk = hgu.shape[1] // LANES
            gate = jnp.concatenate([hgu[:, b * LANES:(b + 1) * LANES] for b in range(0, nchunk, 2)], axis=-1)
            up = jnp.concatenate([hgu[:, b * LANES:(b + 1) * LANES] for b in range(1, nchunk, 2)], axis=-1)
            gate = jnp.minimum(gate, SWIGLU_LIMIT)
            up = jnp.clip(up, -SWIGLU_LIMIT, SWIGLU_LIMIT)
            act = (up + 1.0) * (gate * jax.nn.sigmoid(SWIGLU_ALPHA * gate))
            acc_cur[sub_rows(q), :] += jnp.dot(act.astype(BF16), wd_ref[0], preferred_element_type=F32)
            return c

        lax.fori_loop(0, nsub, compute, 0)

        @pl.when(j == nj - 1)
        def _epilogue():
            @pl.when(nsub_prev > 0)
            def _():
                remaining_groups(scatter_prv, nsub)

            started = jnp.where(nsub_prev > 0, MOE_SUBS_PER_BLOCK, nsub)
            wait_blocks(acc_prv, y_hbm, sem_out.at[1 - p], started)

    for p in range(2):
        pl.when(s % 2 == p)(functools.partial(run, p))


def _moe(h2, sb_e, live, rows, wgu, bgu, wdn, bdn):
    n, d = h2.shape
    ne, f = wdn.shape[0], wdn.shape[1]
    nj = f // MOE_TF
    n_sb = sb_e.shape[0]

    def ff(s, j, live_ref):
        return jnp.where(live_ref[s] > 0, j, nj - 1)

    grid_spec = pltpu.PrefetchScalarGridSpec(
        num_scalar_prefetch=2,
        grid=(n_sb, nj),
        in_specs=[pl.BlockSpec(memory_space=pl.ANY),
                  pl.BlockSpec(memory_space=pl.ANY),
                  pl.BlockSpec((1, d, 2 * MOE_TF), lambda s, j, e, ns: (e[s], 0, ff(s, j, ns))),
                  pl.BlockSpec((1, 1, 2 * MOE_TF), lambda s, j, e, ns: (e[s], 0, ff(s, j, ns))),
                  pl.BlockSpec((1, MOE_TF, d), lambda s, j, e, ns: (e[s], ff(s, j, ns), 0)),
                  pl.BlockSpec((1, 1, d), lambda s, j, e, ns: (e[s], 0, 0))],
        out_specs=pl.BlockSpec(memory_space=pl.ANY),
        scratch_shapes=[pltpu.SMEM((4 * 2 * MOE_TM,), jnp.int32),
                        pltpu.VMEM((MOE_TM, d), F32),
                        pltpu.VMEM((MOE_TM, d), F32),
                        pltpu.VMEM((MOE_TM, d), F32),
                        pltpu.VMEM((MOE_TM, d), F32),
                        pltpu.SemaphoreType.DMA(()),
                        pltpu.SemaphoreType.DMA((2,)),
                        pltpu.SemaphoreType.DMA((2,))])
    assert MOE_TM % (MOE_SUBS_PER_BLOCK * nj) == 0 and n_sb >= 3
    return pl.pallas_call(
        functools.partial(_moe_kernel, nj=nj),
        grid_spec=grid_spec,
        out_shape=jax.ShapeDtypeStruct((TOP_K * n + MOE_TM, d), F32),
        compiler_params=_cparams("arbitrary", "arbitrary"),
        name="moe",
    )(sb_e, live, rows, h2, wgu, bgu, wdn, bdn)


def _combine_kernel(x_ref, y0_ref, y1_ref, y2_ref, y3_ref, gate_ref, gfin_ref, *o_refs, final, starts):
    g = gate_ref[...]
    x = x_ref[...]
    for k, y_ref in enumerate((y0_ref, y1_ref, y2_ref, y3_ref)):
        x = x + g[:, k:k + 1] * y_ref[...]
    x = _rms(x, gfin_ref[...]) if final else x
    if len(o_refs) == 1:
        o_refs[0][...] = x
        return
    i = pl.program_id(0)
    bounds = list(starts) + [pl.num_programs(0)]
    for k, o_ref in enumerate(o_refs):
        @pl.when(jnp.logical_and(i >= bounds[k], i < bounds[k + 1]))
        def _(o_ref=o_ref):
            o_ref[...] = x


def _combine(x1, y, gates, gfin, *, final, tm, split=None):
    n, d = x1.shape
    nb = n // tm
    y_spec = lambda k: pl.BlockSpec((tm, d), lambda i: (k * nb + i, 0))
    if split is None:
        starts = (0,)
        out_specs = pl.BlockSpec((tm, d), lambda i: (i, 0))
        out_shape = jax.ShapeDtypeStruct((n, d), F32)
    else:
        starts, out_specs, out_shape, s0 = [], [], [], 0
        for nk in split:
            assert nk % tm == 0
            out_specs.append(pl.BlockSpec((tm, d), lambda i, s0=s0, nbk=nk // tm: (jnp.clip(i - s0, 0, nbk - 1), 0)))
            out_shape.append(jax.ShapeDtypeStruct((nk, d), F32))
            starts.append(s0)
            s0 += nk // tm
        starts = tuple(starts)
    return pl.pallas_call(
        functools.partial(_combine_kernel, final=final, starts=starts),
        grid=(nb,),
        in_specs=[pl.BlockSpec((tm, d), lambda i: (i, 0)), y_spec(0), y_spec(1), y_spec(2), y_spec(3),
                  pl.BlockSpec((tm, TOP_K), lambda i: (i, 0)),
                  pl.BlockSpec((1, d), lambda i: (0, 0))],
        out_specs=out_specs,
        out_shape=out_shape,
        compiler_params=_cparams("arbitrary"),
        name="combine",
    )(x1, y, y, y, y, gates, gfin.reshape(1, d))


def _pick(n, *cands):
    for c in cands:
        if n % c == 0:
            return c
    raise ValueError(f"no tile for {n}")


def kernel(x_prompt, x_sample, g_mix, w_in, rpb, conv_w, conv_b, w_rg_a, b_rg_a, w_rg_x, b_rg_x, rg_lambda,
           g_attn_out, g_lru_out, w_out, g_ffn, w_router, b_router, w_gate_up, b_gate_up, w_down, b_down, g_final):
    depth = w_in.shape[0]
    d = x_prompt.shape[-1]
    attn_w = rpb.shape[1] * HEAD_DIM
    lru_w = conv_w.shape[-1]
    ne = w_router.shape[-1]
    f = w_down.shape[2]
    seqs = []
    off = 0
    for xs in (x_prompt, x_sample):
        seqs.append((off, xs.shape[0], xs.shape[1]))
        off += xs.shape[0] * xs.shape[1]
    n = off
    x = jnp.concatenate([x_prompt.reshape(-1, d), x_sample.reshape(-1, d)], axis=0)
    assert attn_w % LANES == 0 and lru_w == attn_w, "column blocks of z assume equal attention / recurrent widths"

    for l in range(depth):
        z = _inproj(x, g_mix[l], w_in[l].astype(BF16), tm=_pick(n, 1024, 512, 256), tn=_pick(w_in.shape[2], 1024, 512, 256))
        bias = _na_bias_table(rpb[l])
        wa, wx = w_rg_a[l].astype(BF16), w_rg_x[l].astype(BF16)
        attn, h = [], []
        for (o, b, t) in seqs:
            attn.append(_na(z, bias, row_off=o, batch=b, t=t, attn_w=attn_w, tq=_pick(t, 512, 256, 128, 64)))
            lru_args = (z, conv_w[l], conv_b[l], wa, b_rg_a[l], wx, b_rg_x[l], rg_lambda[l])
            if b == SUBLANES and n % t == 0 and o % (b * t) == 0:
                h.append(_lru_batched(*lru_args, row_off=o, t=t, col_blk=3, tt=_pick(t, 128, 64)))
            else:
                h.append(_lru(*lru_args, row_off=o, batch=b, t=t, col_blk=3, tt=_pick(t, 512, 256, 128, 64)))
        x1, h2, top_idx, gates = _outproj(attn, h, z, x, g_attn_out[l], g_lru_out[l], w_out[l].astype(BF16),
                                          g_ffn[l], w_router[l], b_router[l], g_blk=4, tm=_pick(n, 256, 128))
        sb_e, live, rows = _moe_plan(top_idx, ne)
        wgu = _wprep(w_gate_up, l, regroup=True, tr=d, tn=_pick(2 * f, 1024, 512, 256))
        wdn = _wprep(w_down, l, regroup=False, tr=_pick(f, 512, 256), tn=d)
        bgu = _regroup_gate_up(b_gate_up[l]).reshape(ne, 1, 2 * f)
        y = _moe(h2, sb_e, live, rows, wgu, bgu, wdn, b_down[l].reshape(ne, 1, d))
        last = l == depth - 1
        x = _combine(x1, y, gates, g_final, final=last, tm=_pick(n, 256, 128),
                     split=tuple(b * t for (_, b, t) in seqs) if last else None)

    return x[0].reshape(x_prompt.shape), x[1].reshape(x_sample.shape)
```

```python
import functools

import jax
import jax.numpy as jnp
from jax import lax
from jax.experimental import pallas as pl
from jax.experimental.pallas import tpu as pltpu

F32 = jnp.float32
BF16 = jnp.bfloat16

GRID_W = 64
HEAD_DIM = 64
NA_ROWS = 8
NA_COLS = 16
CONV_W = 4
RG_C = 8.0
LRU_BLOCK_W = 128
TOP_K = 4
SWIGLU_LIMIT = 7.0
SWIGLU_ALPHA = 1.702
NORM_EPS = 1e-5
MASK_BIAS = -1e30

LANES = 128
SUBLANES = 8
BF16_SUBLANES = 16
VMEM_LIMIT = 56 * 1024 * 1024

MOE_SUB = 256
MOE_SUBS_PER_BLOCK = 4
MOE_TM = MOE_SUB * MOE_SUBS_PER_BLOCK
MOE_TF = 512
MOE_ISSUE_UNROLL = 4


def _cparams(*sem):
    return pltpu.CompilerParams(dimension_semantics=sem, vmem_limit_bytes=VMEM_LIMIT)


def _rms(x, g):
    return x * lax.rsqrt(jnp.mean(x * x, axis=-1, keepdims=True) + NORM_EPS) * g


def _inproj_kernel(x_ref, g_ref, w_ref, o_ref, xn_ref):
    @pl.when(pl.program_id(1) == 0)
    def _():
        xn_ref[...] = _rms(x_ref[...], g_ref[...]).astype(BF16)

    o_ref[...] = jnp.dot(xn_ref[...], w_ref[...], preferred_element_type=F32).astype(o_ref.dtype)


def _inproj(x, g, w, *, tm, tn):
    n, d = x.shape
    wn = w.shape[1]
    return pl.pallas_call(
        _inproj_kernel,
        grid=(n // tm, wn // tn),
        in_specs=[pl.BlockSpec((tm, d), lambda i, j: (i, 0)),
                  pl.BlockSpec((1, d), lambda i, j: (0, 0)),
                  pl.BlockSpec((d, tn), lambda i, j: (0, j))],
        out_specs=pl.BlockSpec((tm, tn), lambda i, j: (i, j)),
        out_shape=jax.ShapeDtypeStruct((n, wn), BF16),
        scratch_shapes=[pltpu.VMEM((tm, d), BF16)],
        compiler_params=_cparams("arbitrary", "arbitrary"),
        name="inproj",
    )(x, g.reshape(1, d), w)


def _na_bias_table(rpb):
    h = rpb.shape[0]
    c = jnp.arange(GRID_W)
    col_start = jnp.clip(c - NA_COLS // 2, 0, GRID_W - NA_COLS)
    j = jnp.arange(GRID_W)
    inwin = (j[None, :] >= col_start[:, None]) & (j[None, :] < col_start[:, None] + NA_COLS)
    dc = jnp.clip(j[None, :] - c[:, None] + (NA_COLS - 1), 0, 2 * NA_COLS - 2)
    delta = jnp.arange(NA_ROWS)
    kk = jnp.arange(NA_ROWS)
    dr = kk[None, :] - delta[:, None] + (NA_ROWS - 1)
    row_sel = jax.nn.one_hot(dr, 2 * NA_ROWS - 1, dtype=F32)
    col_sel = jax.nn.one_hot(dc, 2 * NA_COLS - 1, dtype=F32)
    t = jnp.einsum("dka,hab,cjb->hdkcj", row_sel, rpb.astype(F32), col_sel, precision=lax.Precision.HIGHEST)
    t = jnp.where(inwin[None, None, None], t, MASK_BIAS)
    t = t.transpose(0, 1, 3, 2, 4).reshape(h // 2, 2, NA_ROWS, GRID_W, NA_ROWS * GRID_W)
    return t.transpose(0, 2, 1, 3, 4).reshape(h // 2, NA_ROWS, 2 * GRID_W, NA_ROWS * GRID_W)


def _na_kernel(q_ref, k_ref, v_ref, b_ref, o_ref, *, rows, rows_per_tile):
    ti = pl.program_id(2)
    lane = lax.broadcasted_iota(jnp.int32, (GRID_W, 2 * HEAD_DIM), 1)
    first_head = lane < HEAD_DIM
    win = NA_ROWS * GRID_W

    def body(rr, carry):
        r = ti * rows_per_tile + rr
        r0 = jnp.clip(r - NA_ROWS // 2, 0, rows - NA_ROWS)
        q = q_ref[pl.ds(pl.multiple_of(rr * GRID_W, GRID_W), GRID_W), :] * (HEAD_DIM ** -0.5)
        zero = jnp.zeros_like(q)
        qm = jnp.concatenate([jnp.where(first_head, q, zero), jnp.where(first_head, zero, q)], axis=0)
        ks = pl.multiple_of(r0 * GRID_W, GRID_W)
        kw = k_ref[pl.ds(ks, win), :]
        vw = v_ref[pl.ds(ks, win), :]
        s = lax.dot_general(qm, kw, (((1,), (1,)), ((), ())), preferred_element_type=F32)
        s = s + b_ref[0, r - r0]
        m = jnp.max(s, axis=-1, keepdims=True)
        p = jnp.exp(s - m)
        l = jnp.sum(p, axis=-1, keepdims=True)
        o = jnp.dot(p.astype(BF16), vw, preferred_element_type=F32) / l
        out = jnp.where(first_head, o[:GRID_W], o[GRID_W:])
        o_ref[pl.ds(pl.multiple_of(rr * GRID_W, GRID_W), GRID_W), :] = out.astype(o_ref.dtype)
        return carry

    lax.fori_loop(0, rows_per_tile, body, 0, unroll=True)


def _na(z, bias, *, row_off, batch, t, attn_w, tq):
    rows = t // GRID_W
    assert rows >= NA_ROWS and t % tq == 0 and row_off % t == 0 and tq % GRID_W == 0
    hp = attn_w // LANES
    nq = t // tq
    qoff, koff = row_off // tq, row_off // t
    kern = functools.partial(_na_kernel, rows=rows, rows_per_tile=tq // GRID_W)
    return pl.pallas_call(
        kern,
        grid=(batch, hp, nq),
        in_specs=[pl.BlockSpec((tq, LANES), lambda b, h, i: (qoff + b * nq + i, h)),
                  pl.BlockSpec((t, LANES), lambda b, h, i: (koff + b, hp + h)),
                  pl.BlockSpec((t, LANES), lambda b, h, i: (koff + b, 2 * hp + h)),
                  pl.BlockSpec((1,) + bias.shape[1:], lambda b, h, i: (h, 0, 0, 0))],
        out_specs=pl.BlockSpec((tq, LANES), lambda b, h, i: (b * nq + i, h)),
        out_shape=jax.ShapeDtypeStruct((batch * t, attn_w), BF16),
        compiler_params=_cparams("arbitrary", "arbitrary", "arbitrary"),
        name="na",
    )(z, z, z, bias)


def _softplus(x):
    return jnp.maximum(x, 0.0) + jnp.log(1.0 + jnp.exp(-jnp.abs(x)))


def _sigmoid(x):
    return 0.5 * jnp.tanh(0.5 * x) + 0.5


def _lru_coeffs(u, prev, nxt, ext, refs, *, d, ci, tt, nt, t, store):
    cw_ref, cb_ref, wa_ref, ba_ref, wx_ref, bx_ref, lam_ref = refs
    halo = SUBLANES
    ext[pl.ds(0, halo), :] = jnp.where(ci == 0, 0.0, prev)
    ext[pl.ds(halo, tt), :] = u
    ext[pl.ds(halo + tt, halo), :] = jnp.where(ci == nt - 1, 0.0, nxt)
    c = cb_ref[...]
    for j in range(CONV_W):
        c = c + cw_ref[pl.ds(j, 1), :] * ext[pl.ds(halo - CONV_W // 2 + j, tt), :]
    cbf = c.astype(BF16)

    tpos = ci * tt + lax.broadcasted_iota(jnp.int32, (tt, 1), 0)
    is_first = tpos == jnp.where(d == 0, 0, t - 1)
    sp = _softplus(-lam_ref[0])
    for n in range(wa_ref.shape[1]):
        sl = slice(n * LRU_BLOCK_W, (n + 1) * LRU_BLOCK_W)
        cn = cbf[:, sl]
        r = _sigmoid(jnp.dot(cn, wa_ref[0, n], preferred_element_type=F32) + ba_ref[0, :, sl])
        ig = _sigmoid(jnp.dot(cn, wx_ref[0, n], preferred_element_type=F32) + bx_ref[0, :, sl])
        a = jnp.exp(-RG_C * r * sp[:, sl])
        mult = jnp.where(is_first, 1.0, jnp.sqrt(1.0 - a * a))
        store(n, a, mult * (ig * c[:, sl]))


def _lru_batched_kernel(u_ref, up_ref, un_ref, cw_ref, cb_ref, wa_ref, ba_ref, wx_ref, bx_ref, lam_ref, o_ref,
                        ext, a_scr, b_scr, carry, *, tt, nt, t):
    d = pl.program_id(0)
    i = pl.program_id(1)
    ci = jnp.where(d == 0, i, nt - 1 - i)
    nblk = a_scr.shape[0]
    refs = (cw_ref, cb_ref, wa_ref, ba_ref, wx_ref, bx_ref, lam_ref)

    def per_sequence(b, c0):
        def store(n, a, bb):
            a_scr[n, :, b, :] = a
            b_scr[n, :, b, :] = bb

        _lru_coeffs(u_ref[b].astype(F32), up_ref[b].astype(F32)[BF16_SUBLANES - SUBLANES:, :],
                    un_ref[b].astype(F32)[:SUBLANES, :], ext, refs, d=d, ci=ci, tt=tt, nt=nt, t=t, store=store)
        return c0

    lax.fori_loop(0, SUBLANES, per_sequence, 0)

    @pl.when(i == 0)
    def _():
        carry[...] = jnp.zeros_like(carry)

    def step(jj, hs):
        j = jnp.where(d == 0, jj, tt - 1 - jj)
        new = []
        for n in range(nblk):
            h = a_scr[n, j] * hs[n] + b_scr[n, j]
            b_scr[n, j] = h
            new.append(h)
        return tuple(new)

    hs = lax.fori_loop(0, tt, step, tuple(carry[n] for n in range(nblk)))
    for n in range(nblk):
        carry[n] = hs[n]

    def write(b, c0):
        for n in range(nblk):
            o_ref[0, n, b] = b_scr[n, :, b, :]
        return c0

    lax.fori_loop(0, SUBLANES, write, 0)


def _lru_batched(z, cw, cb, wa, ba, wx, bx, lam, *, row_off, t, col_blk, tt):
    c = cw.shape[1]
    nblk = c // LRU_BLOCK_W
    nt = t // tt
    ntot = z.shape[0]
    nb = SUBLANES
    assert t % tt == 0 and ntot % t == 0 and row_off % (nb * t) == 0 and tt % BF16_SUBLANES == 0
    z3 = z.reshape(ntot // t, t, z.shape[1])
    g = row_off // (nb * t)
    hb = BF16_SUBLANES

    def chunk(dd, i):
        return jnp.where(dd == 0, i, nt - 1 - i)

    kern = functools.partial(_lru_batched_kernel, tt=tt, nt=nt, t=t)
    dir_w = lambda dd, i: (dd, 0, 0, 0)
    dir_v = lambda dd, i: (dd, 0, 0)
    out = pl.pallas_call(
        kern,
        grid=(2, nt),
        in_specs=[pl.BlockSpec((nb, tt, c), lambda dd, i: (g, chunk(dd, i), col_blk)),
                  pl.BlockSpec((nb, hb, c), lambda dd, i: (g, jnp.maximum(chunk(dd, i) * (tt // hb) - 1, 0), col_blk)),
                  pl.BlockSpec((nb, hb, c), lambda dd, i: (g, jnp.minimum((chunk(dd, i) + 1) * (tt // hb), t // hb - 1), col_blk)),
                  pl.BlockSpec((CONV_W, c), lambda dd, i: (0, 0)),
                  pl.BlockSpec((1, c), lambda dd, i: (0, 0)),
                  pl.BlockSpec((1, nblk, LRU_BLOCK_W, LRU_BLOCK_W), dir_w),
                  pl.BlockSpec((1, 1, c), dir_v),
                  pl.BlockSpec((1, nblk, LRU_BLOCK_W, LRU_BLOCK_W), dir_w),
                  pl.BlockSpec((1, 1, c), dir_v),
                  pl.BlockSpec((1, 1, c), dir_v)],
        out_specs=pl.BlockSpec((1, nblk, nb, tt, LRU_BLOCK_W), lambda dd, i: (dd, 0, 0, chunk(dd, i), 0)),
        out_shape=jax.ShapeDtypeStruct((2, nblk, nb, t, LRU_BLOCK_W), F32),
        scratch_shapes=[pltpu.VMEM((tt + 2 * SUBLANES, c), F32),
                        pltpu.VMEM((nblk, tt, nb, LRU_BLOCK_W), F32),
                        pltpu.VMEM((nblk, tt, nb, LRU_BLOCK_W), F32),
                        pltpu.VMEM((nblk, nb, LRU_BLOCK_W), F32)],
        compiler_params=_cparams("arbitrary", "arbitrary"),
        name="lru_batched",
    )(z3, z3, z3, cw, cb.reshape(1, c), wa, ba.reshape(2, 1, c), wx, bx.reshape(2, 1, c), lam.reshape(2, 1, c))
    return out.reshape(2, nblk, nb * t, LRU_BLOCK_W)


def _lru_kernel(u_ref, up_ref, un_ref, cw_ref, cb_ref, wa_ref, ba_ref, wx_ref, bx_ref, lam_ref, o_ref,
                ext, a_scr, b_scr, carry, *, tt, nt, t):
    d = pl.program_id(1)
    i = pl.program_id(2)
    ci = jnp.where(d == 0, i, nt - 1 - i)
    nblk = a_scr.shape[0]
    seg = tt // SUBLANES

    def store(n, a, bb):
        a_scr[n] = a
        b_scr[n] = bb

    _lru_coeffs(u_ref[...].astype(F32), up_ref[...].astype(F32)[BF16_SUBLANES - SUBLANES:, :],
                un_ref[...].astype(F32)[:SUBLANES, :], ext,
                (cw_ref, cb_ref, wa_ref, ba_ref, wx_ref, bx_ref, lam_ref), d=d, ci=ci, tt=tt, nt=nt, t=t, store=store)

    @pl.when(i == 0)
    def _():
        carry[...] = jnp.zeros_like(carry)

    def step_index(jj):
        return jnp.where(d == 0, jj, seg - 1 - jj)

    def load(ref, n, j):
        return ref[n, pl.ds(j, SUBLANES, stride=seg), :]

    def local_step(jj, hp):
        j = step_index(jj)
        hs, ps = hp
        new_h, new_p = [], []
        for n in range(nblk):
            a = load(a_scr, n, j)
            new_h.append(a * hs[n] + load(b_scr, n, j))
            new_p.append(a * ps[n])
        return tuple(new_h), tuple(new_p)

    zeros = tuple(jnp.zeros((SUBLANES, LRU_BLOCK_W), F32) for _ in range(nblk))
    ones = tuple(jnp.ones((SUBLANES, LRU_BLOCK_W), F32) for _ in range(nblk))
    h_end, p_end = lax.fori_loop(0, seg, local_step, (zeros, ones))

    cin, cout = [], []
    for n in range(nblk):
        c0 = carry[pl.ds(n, 1), :]
        fwd, cur = [], c0
        for s in range(SUBLANES):
            fwd.append(cur)
            cur = h_end[n][s:s + 1] + p_end[n][s:s + 1] * cur
        fwd_out = cur
        bwd, cur = [None] * SUBLANES, c0
        for s in reversed(range(SUBLANES)):
            bwd[s] = cur
            cur = h_end[n][s:s + 1] + p_end[n][s:s + 1] * cur
        bwd_out = cur
        cin.append(jnp.where(d == 0, jnp.concatenate(fwd, axis=0), jnp.concatenate(bwd, axis=0)))
        cout.append(jnp.where(d == 0, fwd_out, bwd_out))

    def final_step(jj, hs):
        j = step_index(jj)
        new_h = []
        for n in range(nblk):
            h = load(a_scr, n, j) * hs[n] + load(b_scr, n, j)
            o_ref[0, n, pl.ds(j, SUBLANES, stride=seg), :] = h
            new_h.append(h)
        return tuple(new_h)

    lax.fori_loop(0, seg, final_step, tuple(cin))
    for n in range(nblk):
        carry[pl.ds(n, 1), :] = cout[n]


def _lru(z, cw, cb, wa, ba, wx, bx, lam, *, row_off, batch, t, col_blk, tt):
    c = cw.shape[1]
    nblk = c // LRU_BLOCK_W
    nt = t // tt
    ntot = z.shape[0]
    assert t % tt == 0 and row_off % tt == 0 and tt % (SUBLANES * SUBLANES) == 0
    hb = BF16_SUBLANES
    base = row_off // tt

    def chunk(dd, i):
        return jnp.where(dd == 0, i, nt - 1 - i)

    def u_map(b, dd, i):
        return (base + b * nt + chunk(dd, i), col_blk)

    def prev_map(b, dd, i):
        return (jnp.maximum((row_off + b * t + chunk(dd, i) * tt) // hb - 1, 0), col_blk)

    def next_map(b, dd, i):
        return (jnp.minimum((row_off + b * t + (chunk(dd, i) + 1) * tt) // hb, ntot // hb - 1), col_blk)

    kern = functools.partial(_lru_kernel, tt=tt, nt=nt, t=t)
    dir_w = lambda b, dd, i: (dd, 0, 0, 0)
    dir_v = lambda b, dd, i: (dd, 0, 0)
    return pl.pallas_call(
        kern,
        grid=(batch, 2, nt),
        in_specs=[pl.BlockSpec((tt, c), u_map),
                  pl.BlockSpec((hb, c), prev_map),
                  pl.BlockSpec((hb, c), next_map),
                  pl.BlockSpec((CONV_W, c), lambda b, dd, i: (0, 0)),
                  pl.BlockSpec((1, c), lambda b, dd, i: (0, 0)),
                  pl.BlockSpec((1, nblk, LRU_BLOCK_W, LRU_BLOCK_W), dir_w),
                  pl.BlockSpec((1, 1, c), dir_v),
                  pl.BlockSpec((1, nblk, LRU_BLOCK_W, LRU_BLOCK_W), dir_w),
                  pl.BlockSpec((1, 1, c), dir_v),
                  pl.BlockSpec((1, 1, c), dir_v)],
        out_specs=pl.BlockSpec((1, nblk, tt, LRU_BLOCK_W), lambda b, dd, i: (dd, 0, b * nt + chunk(dd, i), 0)),
        out_shape=jax.ShapeDtypeStruct((2, nblk, batch * t, LRU_BLOCK_W), F32),
        scratch_shapes=[pltpu.VMEM((tt + 2 * SUBLANES, c), F32),
                        pltpu.VMEM((nblk, tt, LRU_BLOCK_W), F32),
                        pltpu.VMEM((nblk, tt, LRU_BLOCK_W), F32),
                        pltpu.VMEM((nblk, LRU_BLOCK_W), F32)],
        compiler_params=_cparams("arbitrary", "arbitrary", "arbitrary"),
        name="lru",
    )(z, z, z, cw, cb.reshape(1, c), wa, ba.reshape(2, 1, c), wx, bx.reshape(2, 1, c), lam.reshape(2, 1, c))


def _gelu_tanh(x):
    return 0.5 * x * (1.0 + jnp.tanh(0.7978845608028654 * (x + 0.044715 * (x * x * x))))


def _outproj_kernel(*refs, starts):
    ng = len(starts)
    groups = [refs[3 * k:3 * k + 3] for k in range(ng)]
    (g_ref, x_ref, ga_ref, gl_ref, wo_ref, gf_ref, wrh_ref, wrl_ref, br_ref,
     x1_ref, h2_ref, idx_ref, gate_ref, x1_prev) = refs[3 * ng:]
    i = pl.program_id(0)
    nblk = groups[0][1].shape[1]

    @pl.when(i == 0)
    def _():
        x1_prev[...] = jnp.zeros_like(x1_prev)

    h2 = _rms(x1_prev[...], gf_ref[...])
    h2_ref[...] = h2
    h2_hi = h2.astype(BF16)
    h2_lo = (h2 - h2_hi.astype(F32)).astype(BF16)
    logits = (jnp.dot(h2_hi, wrh_ref[...], preferred_element_type=F32)
              + jnp.dot(h2_lo, wrh_ref[...], preferred_element_type=F32)
              + jnp.dot(h2_hi, wrl_ref[...], preferred_element_type=F32)) + br_ref[...]
    ne = logits.shape[-1]
    eidx = lax.broadcasted_iota(jnp.int32, logits.shape, 1)
    kidx = lax.broadcasted_iota(jnp.int32, idx_ref.shape, 1)
    vals = jnp.zeros(gate_ref.shape, F32)
    idxs = jnp.zeros(idx_ref.shape, jnp.int32)
    cur = logits
    for k in range(TOP_K):
        m = jnp.max(cur, axis=-1, keepdims=True)
        sel = jnp.min(jnp.where(cur == m, eidx, ne), axis=-1, keepdims=True)
        vals = jnp.where(kidx == k, m, vals)
        idxs = jnp.where(kidx == k, sel, idxs)
        cur = jnp.where(eidx == sel, -jnp.inf, cur)
    e = jnp.exp(vals - vals[:, 0:1])
    gate_ref[...] = e / jnp.sum(e, axis=-1, keepdims=True)
    idx_ref[...] = idxs


    def read(k):
        attn_ref, hf_ref, hb_ref = groups[k]
        return (attn_ref[...].astype(F32),
                jnp.concatenate([hf_ref[0, n] + hb_ref[0, n] for n in range(nblk)], axis=-1))

    attn, h = read(0)
    for k in range(1, ng):
        attn_k, h_k = read(k)
        attn = jnp.where(i >= starts[k], attn_k, attn)
        h = jnp.where(i >= starts[k], h_k, h)
    an = _rms(attn, ga_ref[...])
    rn = _rms(h * _gelu_tanh(g_ref[...].astype(F32)), gl_ref[...])
    mixed = jnp.concatenate([an, rn], axis=-1).astype(BF16)
    x1 = x_ref[...] + jnp.dot(mixed, wo_ref[...], preferred_element_type=F32)
    x1_ref[...] = x1
    x1_prev[...] = x1


def _outproj(attns, hs, z, x, ga, gl, wo, gf, wr, br, *, g_blk, tm):
    n, d = x.shape
    aw = attns[0].shape[1]
    nblk = hs[0].shape[1]
    c = nblk * LRU_BLOCK_W
    ne = wr.shape[1]
    nt = n // tm
    wr_hi = wr.astype(BF16)
    wr_lo = (wr - wr_hi.astype(F32)).astype(BF16)
    const = lambda i: (0, 0)
    cur = lambda i: jnp.minimum(i, nt - 1)
    prv = lambda i: jnp.maximum(i - 1, 0)
    starts, group_specs, group_args = [], [], []
    s0 = 0
    for attn, h in zip(attns, hs):
        nb = attn.shape[0] // tm
        assert attn.shape[0] % tm == 0
        local = lambda i, s0=s0, nb=nb: jnp.clip(i - s0, 0, nb - 1)
        group_specs += [pl.BlockSpec((tm, aw), lambda i, f=local: (f(i), 0)),
                        pl.BlockSpec((1, nblk, tm, LRU_BLOCK_W), lambda i, f=local: (0, 0, f(i), 0)),
                        pl.BlockSpec((1, nblk, tm, LRU_BLOCK_W), lambda i, f=local: (1, 0, f(i), 0))]
        group_args += [attn, h, h]
        starts.append(s0)
        s0 += nb
    assert s0 == nt
    return pl.pallas_call(
        functools.partial(_outproj_kernel, starts=tuple(starts)),
        grid=(nt + 1,),
        in_specs=group_specs + [
                  pl.BlockSpec((tm, c), lambda i: (cur(i), g_blk)),
                  pl.BlockSpec((tm, d), lambda i: (cur(i), 0)),
                  pl.BlockSpec((1, aw), const),
                  pl.BlockSpec((1, c), const),
                  pl.BlockSpec((d, d), const),
                  pl.BlockSpec((1, d), const),
                  pl.BlockSpec((d, ne), const),
                  pl.BlockSpec((d, ne), const),
                  pl.BlockSpec((1, ne), const)],
        out_specs=[pl.BlockSpec((tm, d), lambda i: (cur(i), 0)),
                   pl.BlockSpec((tm, d), lambda i: (prv(i), 0)),
                   pl.BlockSpec((tm, TOP_K), lambda i: (prv(i), 0)),
                   pl.BlockSpec((tm, TOP_K), lambda i: (prv(i), 0))],
        out_shape=[jax.ShapeDtypeStruct((n, d), F32),
                   jax.ShapeDtypeStruct((n, d), F32),
                   jax.ShapeDtypeStruct((n, TOP_K), jnp.int32),
                   jax.ShapeDtypeStruct((n, TOP_K), F32)],
        scratch_shapes=[pltpu.VMEM((tm, d), F32)],
        compiler_params=_cparams("arbitrary"),
        name="outproj",
    )(*group_args, z, x, ga.reshape(1, aw), gl.reshape(1, c), wo, gf.reshape(1, d), wr_hi, wr_lo, br.reshape(1, ne))


def _moe_plan(top_idx, n_experts):
    n = top_idx.shape[0]
    m = n * TOP_K
    n_sb = -(-m // MOE_TM) + n_experts + 1
    flat_e = top_idx.reshape(-1)
    order = jnp.argsort(flat_e).astype(jnp.int32)
    counts = jnp.sum((flat_e[:, None] == jnp.arange(n_experts)[None, :]).astype(jnp.int32), axis=0)
    start = jnp.cumsum(counts) - counts
    sb_per_e = (counts + MOE_TM - 1) // MOE_TM
    sb_end = jnp.cumsum(sb_per_e)
    sb = jnp.arange(n_sb, dtype=jnp.int32)
    sb_e = jnp.minimum(jnp.sum((sb_end[None, :] <= sb[:, None]).astype(jnp.int32), axis=1), n_experts - 1)
    local = sb - (sb_end - sb_per_e)[sb_e]
    valid = jnp.clip(counts[sb_e] - local * MOE_TM, 0, MOE_TM)
    valid = jnp.where(sb < sb_end[-1], valid, 0).astype(jnp.int32)
    r = jnp.arange(MOE_TM, dtype=jnp.int32)
    live = r[None, :] < valid[:, None]
    sorted_pos = jnp.clip(start[sb_e][:, None] + local[:, None] * MOE_TM + r[None, :], 0, m - 1)
    a = order[sorted_pos]
    src = jnp.where(live, a // TOP_K, 0)
    dst = jnp.where(live, (a % TOP_K) * n + a // TOP_K, m + r[None, :])
    rows = jnp.concatenate([src, dst], axis=1).reshape(-1).astype(jnp.int32)
    return sb_e, valid, rows


def _wprep_kernel(w_ref, o_ref):
    grp = 2 * LANES
    r = lax.broadcasted_iota(jnp.int32, (grp, grp), 0)
    c = lax.broadcasted_iota(jnp.int32, (grp, grp), 1)
    perm = (r == jnp.where(c < LANES, 2 * c, 2 * (c - LANES) + 1)).astype(BF16)
    for g in range(w_ref.shape[3] // grp):
        w = w_ref[0, 0, :, g * grp:(g + 1) * grp].astype(BF16)
        o_ref[0, :, g * grp:(g + 1) * grp] = jnp.dot(w, perm, preferred_element_type=F32).astype(BF16)


def _wcast_kernel(w_ref, o_ref):
    o_ref[0] = w_ref[0, 0].astype(BF16)


def _wprep(w, layer, *, regroup, tr, tn):
    _, ne, r, c = w.shape
    return pl.pallas_call(
        _wprep_kernel if regroup else _wcast_kernel,
        grid=(ne, r // tr, c // tn),
        in_specs=[pl.BlockSpec((1, 1, tr, tn), lambda e, i, j: (layer, e, i, j))],
        out_specs=pl.BlockSpec((1, tr, tn), lambda e, i, j: (e, i, j)),
        out_shape=jax.ShapeDtypeStruct((ne, r, c), BF16),
        compiler_params=_cparams("arbitrary", "arbitrary", "arbitrary"),
        name="wprep" if regroup else "wcast",
    )(w)


def _regroup_gate_up(b):
    lead = b.shape[:-1]
    return b.reshape(lead + (-1, LANES, 2)).swapaxes(-1, -2).reshape(lead + (-1,))


def _moe_kernel(sbe_ref, live_ref, rows_hbm, h_hbm, wgu_ref, bgu_ref, wd_ref, bd_ref, y_hbm,
                rows_smem, xg0, xg1, acc0, acc1, sem_idx, sem_in, sem_out, *, nj):
    s = pl.program_id(0)
    j = pl.program_id(1)
    n_sb = pl.num_programs(0)
    xgs, accs = (xg0, xg1), (acc0, acc1)
    nsub = (live_ref[s] + MOE_SUB - 1) // MOE_SUB
    nsub_prev = jnp.where(s > 0, (live_ref[jnp.maximum(s - 1, 0)] + MOE_SUB - 1) // MOE_SUB, 0)
    grp_rows = MOE_TM // (MOE_SUBS_PER_BLOCK * nj)
    plan_words = 2 * MOE_TM

    def sub_rows(q, base=0):
        return pl.ds(pl.multiple_of(base + q * MOE_SUB, MOE_SUB), MOE_SUB)

    def start_rows(start_row, lo, hi):
        ngrp = (hi - lo) // MOE_ISSUE_UNROLL

        def group(g, c):
            for u in range(MOE_ISSUE_UNROLL):
                start_row(lo + g * MOE_ISSUE_UNROLL + u)
            return c

        def single(row, c):
            start_row(row)
            return c

        lax.fori_loop(0, ngrp, group, 0)
        lax.fori_loop(lo + ngrp * MOE_ISSUE_UNROLL, hi, single, 0)

    def wait_blocks(src, dst, sem, nblocks):
        def block(q, c):
            pltpu.make_async_copy(src.at[pl.ds(0, MOE_SUB), :], dst.at[pl.ds(0, MOE_SUB), :], sem).wait()
            return c

        lax.fori_loop(0, nblocks, block, 0)

    def idx_copy(sb, slot_of=None):
        slot = ((sb if slot_of is None else slot_of) % 4) * plan_words
        return pltpu.make_async_copy(rows_hbm.at[pl.ds(pl.multiple_of(sb * plan_words, plan_words), plan_words)],
                                     rows_smem.at[pl.ds(pl.multiple_of(slot, plan_words), plan_words)], sem_idx)

    def gather_row(sb, buf, sem, row):
        tok = rows_smem[(sb % 4) * plan_words + row]
        pltpu.make_async_copy(h_hbm.at[pl.ds(tok, 1), :], buf.at[pl.ds(row, 1), :], sem).start()

    def scatter_prev_row(buf, sem, row):
        dst = rows_smem[((s + 3) % 4) * plan_words + MOE_TM + row]
        pltpu.make_async_copy(buf.at[pl.ds(row, 1), :], y_hbm.at[pl.ds(dst, 1), :], sem).start()

    def remaining_groups(start_row, nsub_done):
        def per_j(jj, c):
            def per_q(q, c2):
                lo = (jj * MOE_SUBS_PER_BLOCK + q) * grp_rows
                start_rows(start_row, lo, lo + grp_rows)
                return c2

            lax.fori_loop(nsub_done, MOE_SUBS_PER_BLOCK, per_q, 0)
            return c

        lax.fori_loop(0, nj, per_j, 0)

    def run(p):
        xg_cur, xg_nxt = xgs[p], xgs[1 - p]
        acc_cur, acc_prv = accs[p], accs[1 - p]
        gather_cur = functools.partial(gather_row, s, xg_cur, sem_in.at[p])
        gather_nxt = functools.partial(gather_row, s + 1, xg_nxt, sem_in.at[1 - p])
        scatter_prv = functools.partial(scatter_prev_row, acc_prv, sem_out.at[1 - p])

        @pl.when(j == 0)
        def _prologue():
            if p == 0:
                @pl.when(s == 0)
                def _():
                    for sb in range(2):
                        idx_copy(sb).start()
                        idx_copy(sb).wait()
                    idx_copy(n_sb - 1, slot_of=3).start()
                    idx_copy(n_sb - 1, slot_of=3).wait()
                    acc_prv[...] = jnp.zeros_like(acc_prv)
                    fill = pltpu.make_async_copy(acc_prv, y_hbm.at[pl.ds(y_hbm.shape[0] - MOE_TM, MOE_TM), :],
                                                 sem_out.at[0])
                    fill.start()
                    fill.wait()

            @pl.when(jnp.logical_and(s > 0, s + 1 < n_sb))
            def _():
                idx_copy(s + 1).wait()

            @pl.when(s + 2 < n_sb)
            def _():
                idx_copy(s + 2).start()

            @pl.when(nsub > 0)
            def _():
                remaining_groups(gather_cur, nsub_prev)

            started = jnp.where(nsub > 0, MOE_SUBS_PER_BLOCK, nsub_prev)
            wait_blocks(h_hbm, xg_cur, sem_in.at[p], started)

            def init(q, c):
                acc_cur[sub_rows(q), :] = jnp.broadcast_to(bd_ref[0], (MOE_SUB, acc_cur.shape[1]))
                return c

            lax.fori_loop(0, MOE_SUBS_PER_BLOCK, init, 0)

        def compute(q, c):
            lo = pl.multiple_of((j * MOE_SUBS_PER_BLOCK + q) * grp_rows, grp_rows)
            for u in range(grp_rows):
                gather_nxt(lo + u)
            x = xg_cur[sub_rows(q), :].astype(BF16)
            hgu = jnp.dot(x, wgu_ref[0], preferred_element_type=F32) + bgu_ref[0]
            nchunk = hgu.shape[1] // LANES
            gate = jnp.concatenate([hgu[:, b * LANES:(b + 1) * LANES] for b in range(0, nchunk, 2)], axis=-1)
            up = jnp.concatenate([hgu[:, b * LANES:(b + 1) * LANES] for b in range(1, nchunk, 2)], axis=-1)
            gate = jnp.minimum(gate, SWIGLU_LIMIT)
            up = jnp.clip(up, -SWIGLU_LIMIT, SWIGLU_LIMIT)
            act = (up + 1.0) * (gate * jax.nn.sigmoid(SWIGLU_ALPHA * gate))
            for u in range(grp_rows):
                scatter_prv(lo + u)
            acc_cur[sub_rows(q), :] += jnp.dot(act.astype(BF16), wd_ref[0], preferred_element_type=F32)
            return c

        lax.fori_loop(0, nsub, compute, 0)

        @pl.when(j == nj - 1)
        def _epilogue():
            @pl.when(nsub_prev > 0)
            def _():
                remaining_groups(scatter_prv, nsub)

            started = jnp.where(nsub_prev > 0, MOE_SUBS_PER_BLOCK, nsub)
            wait_blocks(acc_prv, y_hbm, sem_out.at[1 - p], started)

    for p in range(2):
        pl.when(s % 2 == p)(functools.partial(run, p))


def _moe(h2, sb_e, live, rows, wgu, bgu, wdn, bdn):
    n, d = h2.shape
    ne, f = wdn.shape[0], wdn.shape[1]
    nj = f // MOE_TF
    n_sb = sb_e.shape[0]

    def ff(s, j, live_ref):
        return jnp.where(live_ref[s] > 0, j, nj - 1)

    grid_spec = pltpu.PrefetchScalarGridSpec(
        num_scalar_prefetch=2,
        grid=(n_sb, nj),
        in_specs=[pl.BlockSpec(memory_space=pl.ANY),
                  pl.BlockSpec(memory_space=pl.ANY),
                  pl.BlockSpec((1, d, 2 * MOE_TF), lambda s, j, e, ns: (e[s], 0, ff(s, j, ns))),
                  pl.BlockSpec((1, 1, 2 * MOE_TF), lambda s, j, e, ns: (e[s], 0, ff(s, j, ns))),
                  pl.BlockSpec((1, MOE_TF, d), lambda s, j, e, ns: (e[s], ff(s, j, ns), 0)),
                  pl.BlockSpec((1, 1, d), lambda s, j, e, ns: (e[s], 0, 0))],
        out_specs=pl.BlockSpec(memory_space=pl.ANY),
        scratch_shapes=[pltpu.SMEM((4 * 2 * MOE_TM,), jnp.int32),
                        pltpu.VMEM((MOE_TM, d), F32),
                        pltpu.VMEM((MOE_TM, d), F32),
                        pltpu.VMEM((MOE_TM, d), F32),
                        pltpu.VMEM((MOE_TM, d), F32),
                        pltpu.SemaphoreType.DMA(()),
                        pltpu.SemaphoreType.DMA((2,)),
                        pltpu.SemaphoreType.DMA((2,))])
    assert MOE_TM % (MOE_SUBS_PER_BLOCK * nj) == 0 and n_sb >= 3
    return pl.pallas_call(
        functools.partial(_moe_kernel, nj=nj),
        grid_spec=grid_spec,
        out_shape=jax.ShapeDtypeStruct((TOP_K * n + MOE_TM, d), F32),
        compiler_params=_cparams("arbitrary", "arbitrary"),
        name="moe",
    )(sb_e, live, rows, h2, wgu, bgu, wdn, bdn)


def _combine_kernel(x_ref, y0_ref, y1_ref, y2_ref, y3_ref, gate_ref, gfin_ref, *o_refs, final, starts):
    g = gate_ref[...]
    x = x_ref[...]
    for k, y_ref in enumerate((y0_ref, y1_ref, y2_ref, y3_ref)):
        x = x + g[:, k:k + 1] * y_ref[...]
    x = _rms(x, gfin_ref[...]) if final else x
    if len(o_refs) == 1:
        o_refs[0][...] = x
        return
    i = pl.program_id(0)
    bounds = list(starts) + [pl.num_programs(0)]
    for k, o_ref in enumerate(o_refs):
        @pl.when(jnp.logical_and(i >= bounds[k], i < bounds[k + 1]))
        def _(o_ref=o_ref):
            o_ref[...] = x


def _combine(x1, y, gates, gfin, *, final, tm, split=None):
    n, d = x1.shape
    nb = n // tm
    y_spec = lambda k: pl.BlockSpec((tm, d), lambda i: (k * nb + i, 0))
    if split is None:
        starts = (0,)
        out_specs = pl.BlockSpec((tm, d), lambda i: (i, 0))
        out_shape = jax.ShapeDtypeStruct((n, d), F32)
    else:
        starts, out_specs, out_shape, s0 = [], [], [], 0
        for nk in split:
            assert nk % tm == 0
            out_specs.append(pl.BlockSpec((tm, d), lambda i, s0=s0, nbk=nk // tm: (jnp.clip(i - s0, 0, nbk - 1), 0)))
            out_shape.append(jax.ShapeDtypeStruct((nk, d), F32))
            starts.append(s0)
            s0 += nk // tm
        starts = tuple(starts)
    return pl.pallas_call(
        functools.partial(_combine_kernel, final=final, starts=starts),
        grid=(nb,),
        in_specs=[pl.BlockSpec((tm, d), lambda i: (i, 0)), y_spec(0), y_spec(1), y_spec(2), y_spec(3),
                  pl.BlockSpec((tm, TOP_K), lambda i: (i, 0)),
                  pl.BlockSpec((1, d), lambda i: (0, 0))],
        out_specs=out_specs,
        out_shape=out_shape,
        compiler_params=_cparams("arbitrary"),
        name="combine",
    )(x1, y, y, y, y, gates, gfin.reshape(1, d))


def _pick(n, *cands):
    for c in cands:
        if n % c == 0:
            return c
    raise ValueError(f"no tile for {n}")


def kernel(x_prompt, x_sample, g_mix, w_in, rpb, conv_w, conv_b, w_rg_a, b_rg_a, w_rg_x, b_rg_x, rg_lambda,
           g_attn_out, g_lru_out, w_out, g_ffn, w_router, b_router, w_gate_up, b_gate_up, w_down, b_down, g_final):
    depth = w_in.shape[0]
    d = x_prompt.shape[-1]
    attn_w = rpb.shape[1] * HEAD_DIM
    lru_w = conv_w.shape[-1]
    ne = w_router.shape[-1]
    f = w_down.shape[2]
    seqs = []
    off = 0
    for xs in (x_prompt, x_sample):
        seqs.append((off, xs.shape[0], xs.shape[1]))
        off += xs.shape[0] * xs.shape[1]
    n = off
    x = jnp.concatenate([x_prompt.reshape(-1, d), x_sample.reshape(-1, d)], axis=0)
    assert attn_w % LANES == 0 and lru_w == attn_w, "column blocks of z assume equal attention / recurrent widths"

    for l in range(depth):
        z = _inproj(x, g_mix[l], w_in[l].astype(BF16), tm=_pick(n, 1024, 512, 256), tn=_pick(w_in.shape[2], 1024, 512, 256))
        bias = _na_bias_table(rpb[l])
        wa, wx = w_rg_a[l].astype(BF16), w_rg_x[l].astype(BF16)
        attn, h = [], []
        for (o, b, t) in seqs:
            attn.append(_na(z, bias, row_off=o, batch=b, t=t, attn_w=attn_w, tq=_pick(t, 512, 256, 128, 64)))
            lru_args = (z, conv_w[l], conv_b[l], wa, b_rg_a[l], wx, b_rg_x[l], rg_lambda[l])
            if b == SUBLANES and n % t == 0 and o % (b * t) == 0:
                h.append(_lru_batched(*lru_args, row_off=o, t=t, col_blk=3, tt=_pick(t, 128, 64)))
            else:
                h.append(_lru(*lru_args, row_off=o, batch=b, t=t, col_blk=3, tt=_pick(t, 512, 256, 128, 64)))
        x1, h2, top_idx, gates = _outproj(attn, h, z, x, g_attn_out[l], g_lru_out[l], w_out[l].astype(BF16),
                                          g_ffn[l], w_router[l], b_router[l], g_blk=4, tm=_pick(n, 256, 128))
        sb_e, live, rows = _moe_plan(top_idx, ne)
        wgu = _wprep(w_gate_up, l, regroup=True, tr=d, tn=_pick(2 * f, 1024, 512, 256))
        wdn = _wprep(w_down, l, regroup=False, tr=_pick(f, 512, 256), tn=d)
        bgu = _regroup_gate_up(b_gate_up[l]).reshape(ne, 1, 2 * f)
        y = _moe(h2, sb_e, live, rows, wgu, bgu, wdn, b_down[l].reshape(ne, 1, d))
        last = l == depth - 1
        x = _combine(x1, y, gates, g_final, final=last, tm=_pick(n, 256, 128),
                     split=tuple(b * t for (_, b, t) in seqs) if last else None)

    return x[0].reshape(x_prompt.shape), x[1].reshape(x_sample.shape)
```

```python
import functools

import jax
import jax.numpy as jnp
from jax import lax
from jax.experimental import pallas as pl
from jax.experimental.pallas import tpu as pltpu

F32 = jnp.float32
BF16 = jnp.bfloat16

GRID_W = 64
HEAD_DIM = 64
NA_ROWS = 8
NA_COLS = 16
CONV_W = 4
RG_C = 8.0
LRU_BLOCK_W = 128
TOP_K = 4
SWIGLU_LIMIT = 7.0
SWIGLU_ALPHA = 1.702
NORM_EPS = 1e-5
MASK_BIAS = -1e30

LANES = 128
SUBLANES = 8
BF16_SUBLANES = 16
VMEM_LIMIT = 56 * 1024 * 1024

MOE_SUB = 256
MOE_SUBS_PER_BLOCK = 4
MOE_TM = MOE_SUB * MOE_SUBS_PER_BLOCK
MOE_TF = 512
MOE_ISSUE_UNROLL = 4


def _cparams(*sem):
    return pltpu.CompilerParams(dimension_semantics=sem, vmem_limit_bytes=VMEM_LIMIT)


def _rms(x, g):
    return x * lax.rsqrt(jnp.mean(x * x, axis=-1, keepdims=True) + NORM_EPS) * g


def _inproj_kernel(x_ref, g_ref, w_ref, o_ref, xn_ref):
    @pl.when(pl.program_id(1) == 0)
    def _():
        xn_ref[...] = _rms(x_ref[...], g_ref[...]).astype(BF16)

    o_ref[...] = jnp.dot(xn_ref[...], w_ref[...], preferred_element_type=F32).astype(o_ref.dtype)


def _inproj(x, g, w, *, tm, tn):
    n, d = x.shape
    wn = w.shape[1]
    return pl.pallas_call(
        _inproj_kernel,
        grid=(n // tm, wn // tn),
        in_specs=[pl.BlockSpec((tm, d), lambda i, j: (i, 0)),
                  pl.BlockSpec((1, d), lambda i, j: (0, 0)),
                  pl.BlockSpec((d, tn), lambda i, j: (0, j))],
        out_specs=pl.BlockSpec((tm, tn), lambda i, j: (i, j)),
        out_shape=jax.ShapeDtypeStruct((n, wn), BF16),
        scratch_shapes=[pltpu.VMEM((tm, d), BF16)],
        compiler_params=_cparams("arbitrary", "arbitrary"),
        name="inproj",
    )(x, g.reshape(1, d), w)


def _na_bias_table(rpb):
    h = rpb.shape[0]
    c = jnp.arange(GRID_W)
    col_start = jnp.clip(c - NA_COLS // 2, 0, GRID_W - NA_COLS)
    j = jnp.arange(GRID_W)
    inwin = (j[None, :] >= col_start[:, None]) & (j[None, :] < col_start[:, None] + NA_COLS)
    dc = jnp.clip(j[None, :] - c[:, None] + (NA_COLS - 1), 0, 2 * NA_COLS - 2)
    delta = jnp.arange(NA_ROWS)
    kk = jnp.arange(NA_ROWS)
    dr = kk[None, :] - delta[:, None] + (NA_ROWS - 1)
    row_sel = jax.nn.one_hot(dr, 2 * NA_ROWS - 1, dtype=F32)
    col_sel = jax.nn.one_hot(dc, 2 * NA_COLS - 1, dtype=F32)
    t = jnp.einsum("dka,hab,cjb->hdkcj", row_sel, rpb.astype(F32), col_sel, precision=lax.Precision.HIGHEST)
    t = jnp.where(inwin[None, None, None], t, MASK_BIAS)
    t = t.transpose(0, 1, 3, 2, 4).reshape(h // 2, 2, NA_ROWS, GRID_W, NA_ROWS * GRID_W)
    return t.transpose(0, 2, 1, 3, 4).reshape(h // 2, NA_ROWS, 2 * GRID_W, NA_ROWS * GRID_W)


def _na_kernel(q_ref, k_ref, v_ref, b_ref, o_ref, s_scr, p_scr, *, rows, rows_per_tile):
    ti = pl.program_id(2)
    lane = lax.broadcasted_iota(jnp.int32, (GRID_W, 2 * HEAD_DIM), 1)
    first_head = lane < HEAD_DIM
    win = NA_ROWS * GRID_W

    def window_start(rr):
        r = ti * rows_per_tile + rr
        r0 = jnp.clip(r - NA_ROWS // 2, 0, rows - NA_ROWS)
        return r - r0, pl.multiple_of(r0 * GRID_W, GRID_W)

    for rr in range(rows_per_tile):
        delta, ks = window_start(rr)
        q = q_ref[pl.ds(rr * GRID_W, GRID_W), :] * (HEAD_DIM ** -0.5)
        zero = jnp.zeros_like(q)
        qm = jnp.concatenate([jnp.where(first_head, q, zero), jnp.where(first_head, zero, q)], axis=0)
        s = lax.dot_general(qm, k_ref[pl.ds(ks, win), :], (((1,), (1,)), ((), ())), preferred_element_type=F32)
        s_scr[rr] = s + b_ref[0, delta]

    for rr in range(rows_per_tile):
        s = s_scr[rr]
        e = jnp.exp(s - jnp.max(s, axis=-1, keepdims=True))
        p_scr[rr] = (e * (1.0 / jnp.sum(e, axis=-1, keepdims=True))).astype(BF16)

    for rr in range(rows_per_tile):
        _, ks = window_start(rr)
        o = jnp.dot(p_scr[rr], v_ref[pl.ds(ks, win), :], preferred_element_type=F32)
        out = jnp.where(first_head, o[:GRID_W], o[GRID_W:])
        o_ref[pl.ds(rr * GRID_W, GRID_W), :] = out.astype(o_ref.dtype)


def _na(z, bias, *, row_off, batch, t, attn_w, tq):
    rows = t // GRID_W
    assert rows >= NA_ROWS and t % tq == 0 and row_off % t == 0 and tq % GRID_W == 0
    hp = attn_w // LANES
    nq = t // tq
    qoff, koff = row_off // tq, row_off // t
    kern = functools.partial(_na_kernel, rows=rows, rows_per_tile=tq // GRID_W)
    return pl.pallas_call(
        kern,
        grid=(batch, hp, nq),
        in_specs=[pl.BlockSpec((tq, LANES), lambda b, h, i: (qoff + b * nq + i, h)),
                  pl.BlockSpec((t, LANES), lambda b, h, i: (koff + b, hp + h)),
                  pl.BlockSpec((t, LANES), lambda b, h, i: (koff + b, 2 * hp + h)),
                  pl.BlockSpec((1,) + bias.shape[1:], lambda b, h, i: (h, 0, 0, 0))],
        out_specs=pl.BlockSpec((tq, LANES), lambda b, h, i: (b * nq + i, h)),
        out_shape=jax.ShapeDtypeStruct((batch * t, attn_w), BF16),
        scratch_shapes=[pltpu.VMEM((tq // GRID_W, 2 * GRID_W, NA_ROWS * GRID_W), F32),
                        pltpu.VMEM((tq // GRID_W, 2 * GRID_W, NA_ROWS * GRID_W), BF16)],
        compiler_params=_cparams("arbitrary", "arbitrary", "arbitrary"),
        name="na",
    )(z, z, z, bias)


def _softplus(x):
    return jnp.maximum(x, 0.0) + jnp.log(1.0 + jnp.exp(-jnp.abs(x)))


def _sigmoid(x):
    return 0.5 * jnp.tanh(0.5 * x) + 0.5


def _lru_coeffs(u, prev, nxt, ext, refs, *, d, ci, tt, nt, t, store):
    cw_ref, cb_ref, wa_ref, ba_ref, wx_ref, bx_ref, lam_ref = refs
    halo = SUBLANES
    ext[pl.ds(0, halo), :] = jnp.where(ci == 0, 0.0, prev)
    ext[pl.ds(halo, tt), :] = u
    ext[pl.ds(halo + tt, halo), :] = jnp.where(ci == nt - 1, 0.0, nxt)
    c = cb_ref[...]
    for j in range(CONV_W):
        c = c + cw_ref[pl.ds(j, 1), :] * ext[pl.ds(halo - CONV_W // 2 + j, tt), :]
    cbf = c.astype(BF16)

    tpos = ci * tt + lax.broadcasted_iota(jnp.int32, (tt, 1), 0)
    is_first = tpos == jnp.where(d == 0, 0, t - 1)
    sp = _softplus(-lam_ref[0])
    for n in range(wa_ref.shape[1]):
        sl = slice(n * LRU_BLOCK_W, (n + 1) * LRU_BLOCK_W)
        cn = cbf[:, sl]
        r = _sigmoid(jnp.dot(cn, wa_ref[0, n], preferred_element_type=F32) + ba_ref[0, :, sl])
        ig = _sigmoid(jnp.dot(cn, wx_ref[0, n], preferred_element_type=F32) + bx_ref[0, :, sl])
        a = jnp.exp(-RG_C * r * sp[:, sl])
        mult = jnp.where(is_first, 1.0, jnp.sqrt(1.0 - a * a))
        store(n, a, mult * (ig * c[:, sl]))


def _lru_batched_kernel(u_ref, up_ref, un_ref, cw_ref, cb_ref, wa_ref, ba_ref, wx_ref, bx_ref, lam_ref, o_ref,
                        ext, a_scr, b_scr, carry, *, tt, nt, t):
    d = pl.program_id(0)
    i = pl.program_id(1)
    ci = jnp.where(d == 0, i, nt - 1 - i)
    nblk = a_scr.shape[0]
    refs = (cw_ref, cb_ref, wa_ref, ba_ref, wx_ref, bx_ref, lam_ref)

    def per_sequence(b, c0):
        def store(n, a, bb):
            a_scr[n, :, b, :] = a
            b_scr[n, :, b, :] = bb

        _lru_coeffs(u_ref[b].astype(F32), up_ref[b].astype(F32)[BF16_SUBLANES - SUBLANES:, :],
                    un_ref[b].astype(F32)[:SUBLANES, :], ext, refs, d=d, ci=ci, tt=tt, nt=nt, t=t, store=store)
        return c0

    lax.fori_loop(0, SUBLANES, per_sequence, 0)

    @pl.when(i == 0)
    def _():
        carry[...] = jnp.zeros_like(carry)

    def step(jj, hs):
        j = jnp.where(d == 0, jj, tt - 1 - jj)
        new = []
        for n in range(nblk):
            h = a_scr[n, j] * hs[n] + b_scr[n, j]
            b_scr[n, j] = h
            new.append(h)
        return tuple(new)

    hs = lax.fori_loop(0, tt, step, tuple(carry[n] for n in range(nblk)))
    for n in range(nblk):
        carry[n] = hs[n]

    def write(b, c0):
        for n in range(nblk):
            o_ref[0, n, b] = b_scr[n, :, b, :]
        return c0

    lax.fori_loop(0, SUBLANES, write, 0)


def _lru_batched(z, cw, cb, wa, ba, wx, bx, lam, *, row_off, t, col_blk, tt):
    c = cw.shape[1]
    nblk = c // LRU_BLOCK_W
    nt = t // tt
    ntot = z.shape[0]
    nb = SUBLANES
    assert t % tt == 0 and ntot % t == 0 and row_off % (nb * t) == 0 and tt % BF16_SUBLANES == 0
    z3 = z.reshape(ntot // t, t, z.shape[1])
    g = row_off // (nb * t)
    hb = BF16_SUBLANES

    def chunk(dd, i):
        return jnp.where(dd == 0, i, nt - 1 - i)

    kern = functools.partial(_lru_batched_kernel, tt=tt, nt=nt, t=t)
    dir_w = lambda dd, i: (dd, 0, 0, 0)
    dir_v = lambda dd, i: (dd, 0, 0)
    out = pl.pallas_call(
        kern,
        grid=(2, nt),
        in_specs=[pl.BlockSpec((nb, tt, c), lambda dd, i: (g, chunk(dd, i), col_blk)),
                  pl.BlockSpec((nb, hb, c), lambda dd, i: (g, jnp.maximum(chunk(dd, i) * (tt // hb) - 1, 0), col_blk)),
                  pl.BlockSpec((nb, hb, c), lambda dd, i: (g, jnp.minimum((chunk(dd, i) + 1) * (tt // hb), t // hb - 1), col_blk)),
                  pl.BlockSpec((CONV_W, c), lambda dd, i: (0, 0)),
                  pl.BlockSpec((1, c), lambda dd, i: (0, 0)),
                  pl.BlockSpec((1, nblk, LRU_BLOCK_W, LRU_BLOCK_W), dir_w),
                  pl.BlockSpec((1, 1, c), dir_v),
                  pl.BlockSpec((1, nblk, LRU_BLOCK_W, LRU_BLOCK_W), dir_w),
                  pl.BlockSpec((1, 1, c), dir_v),
                  pl.BlockSpec((1, 1, c), dir_v)],
        out_specs=pl.BlockSpec((1, nblk, nb, tt, LRU_BLOCK_W), lambda dd, i: (dd, 0, 0, chunk(dd, i), 0)),
        out_shape=jax.ShapeDtypeStruct((2, nblk, nb, t, LRU_BLOCK_W), F32),
        scratch_shapes=[pltpu.VMEM((tt + 2 * SUBLANES, c), F32),
                        pltpu.VMEM((nblk, tt, nb, LRU_BLOCK_W), F32),
                        pltpu.VMEM((nblk, tt, nb, LRU_BLOCK_W), F32),
                        pltpu.VMEM((nblk, nb, LRU_BLOCK_W), F32)],
        compiler_params=_cparams("arbitrary", "arbitrary"),
        name="lru_batched",
    )(z3, z3, z3, cw, cb.reshape(1, c), wa, ba.reshape(2, 1, c), wx, bx.reshape(2, 1, c), lam.reshape(2, 1, c))
    return out.reshape(2, nblk, nb * t, LRU_BLOCK_W)


def _lru_kernel(u_ref, up_ref, un_ref, cw_ref, cb_ref, wa_ref, ba_ref, wx_ref, bx_ref, lam_ref, o_ref,
                ext, a_scr, b_scr, carry, *, tt, nt, t):
    d = pl.program_id(1)
    i = pl.program_id(2)
    ci = jnp.where(d == 0, i, nt - 1 - i)
    nblk = a_scr.shape[0]
    seg = tt // SUBLANES

    def store(n, a, bb):
        a_scr[n] = a
        b_scr[n] = bb

    _lru_coeffs(u_ref[...].astype(F32), up_ref[...].astype(F32)[BF16_SUBLANES - SUBLANES:, :],
                un_ref[...].astype(F32)[:SUBLANES, :], ext,
                (cw_ref, cb_ref, wa_ref, ba_ref, wx_ref, bx_ref, lam_ref), d=d, ci=ci, tt=tt, nt=nt, t=t, store=store)

    @pl.when(i == 0)
    def _():
        carry[...] = jnp.zeros_like(carry)

    def step_index(jj):
        return jnp.where(d == 0, jj, seg - 1 - jj)

    def load(ref, n, j):
        return ref[n, pl.ds(j, SUBLANES, stride=seg), :]

    def local_step(jj, hp):
        j = step_index(jj)
        hs, ps = hp
        new_h, new_p = [], []
        for n in range(nblk):
            a = load(a_scr, n, j)
            new_h.append(a * hs[n] + load(b_scr, n, j))
            new_p.append(a * ps[n])
        return tuple(new_h), tuple(new_p)

    zeros = tuple(jnp.zeros((SUBLANES, LRU_BLOCK_W), F32) for _ in range(nblk))
    ones = tuple(jnp.ones((SUBLANES, LRU_BLOCK_W), F32) for _ in range(nblk))
    h_end, p_end = lax.fori_loop(0, seg, local_step, (zeros, ones))

    cin, cout = [], []
    for n in range(nblk):
        c0 = carry[pl.ds(n, 1), :]
        fwd, cur = [], c0
        for s in range(SUBLANES):
            fwd.append(cur)
            cur = h_end[n][s:s + 1] + p_end[n][s:s + 1] * cur
        fwd_out = cur
        bwd, cur = [None] * SUBLANES, c0
        for s in reversed(range(SUBLANES)):
            bwd[s] = cur
            cur = h_end[n][s:s + 1] + p_end[n][s:s + 1] * cur
        bwd_out = cur
        cin.append(jnp.where(d == 0, jnp.concatenate(fwd, axis=0), jnp.concatenate(bwd, axis=0)))
        cout.append(jnp.where(d == 0, fwd_out, bwd_out))

    def final_step(jj, hs):
        j = step_index(jj)
        new_h = []
        for n in range(nblk):
            h = load(a_scr, n, j) * hs[n] + load(b_scr, n, j)
            o_ref[0, n, pl.ds(j, SUBLANES, stride=seg), :] = h
            new_h.append(h)
        return tuple(new_h)

    lax.fori_loop(0, seg, final_step, tuple(cin))
    for n in range(nblk):
        carry[pl.ds(n, 1), :] = cout[n]


def _lru(z, cw, cb, wa, ba, wx, bx, lam, *, row_off, batch, t, col_blk, tt):
    c = cw.shape[1]
    nblk = c // LRU_BLOCK_W
    nt = t // tt
    ntot = z.shape[0]
    assert t % tt == 0 and row_off % tt == 0 and tt % (SUBLANES * SUBLANES) == 0
    hb = BF16_SUBLANES
    base = row_off // tt

    def chunk(dd, i):
        return jnp.where(dd == 0, i, nt - 1 - i)

    def u_map(b, dd, i):
        return (base + b * nt + chunk(dd, i), col_blk)

    def prev_map(b, dd, i):
        return (jnp.maximum((row_off + b * t + chunk(dd, i) * tt) // hb - 1, 0), col_blk)

    def next_map(b, dd, i):
        return (jnp.minimum((row_off + b * t + (chunk(dd, i) + 1) * tt) // hb, ntot // hb - 1), col_blk)

    kern = functools.partial(_lru_kernel, tt=tt, nt=nt, t=t)
    dir_w = lambda b, dd, i: (dd, 0, 0, 0)
    dir_v = lambda b, dd, i: (dd, 0, 0)
    return pl.pallas_call(
        kern,
        grid=(batch, 2, nt),
        in_specs=[pl.BlockSpec((tt, c), u_map),
                  pl.BlockSpec((hb, c), prev_map),
                  pl.BlockSpec((hb, c), next_map),
                  pl.BlockSpec((CONV_W, c), lambda b, dd, i: (0, 0)),
                  pl.BlockSpec((1, c), lambda b, dd, i: (0, 0)),
                  pl.BlockSpec((1, nblk, LRU_BLOCK_W, LRU_BLOCK_W), dir_w),
                  pl.BlockSpec((1, 1, c), dir_v),
                  pl.BlockSpec((1, nblk, LRU_BLOCK_W, LRU_BLOCK_W), dir_w),
                  pl.BlockSpec((1, 1, c), dir_v),
                  pl.BlockSpec((1, 1, c), dir_v)],
        out_specs=pl.BlockSpec((1, nblk, tt, LRU_BLOCK_W), lambda b, dd, i: (dd, 0, b * nt + chunk(dd, i), 0)),
        out_shape=jax.ShapeDtypeStruct((2, nblk, batch * t, LRU_BLOCK_W), F32),
        scratch_shapes=[pltpu.VMEM((tt + 2 * SUBLANES, c), F32),
                        pltpu.VMEM((nblk, tt, LRU_BLOCK_W), F32),
                        pltpu.VMEM((nblk, tt, LRU_BLOCK_W), F32),
                        pltpu.VMEM((nblk, LRU_BLOCK_W), F32)],
        compiler_params=_cparams("arbitrary", "arbitrary", "arbitrary"),
        name="lru",
    )(z, z, z, cw, cb.reshape(1, c), wa, ba.reshape(2, 1, c), wx, bx.reshape(2, 1, c), lam.reshape(2, 1, c))


def _gelu_tanh(x):
    return 0.5 * x * (1.0 + jnp.tanh(0.7978845608028654 * (x + 0.044715 * (x * x * x))))


def _outproj_kernel(*refs, starts):
    ng = len(starts)
    groups = [refs[3 * k:3 * k + 3] for k in range(ng)]
    (g_ref, x_ref, ga_ref, gl_ref, wo_ref, gf_ref, wrh_ref, wrl_ref, br_ref,
     x1_ref, h2_ref, idx_ref, gate_ref, x1_prev) = refs[3 * ng:]
    i = pl.program_id(0)
    nblk = groups[0][1].shape[1]

    @pl.when(i == 0)
    def _():
        x1_prev[...] = jnp.zeros_like(x1_prev)

    h2 = _rms(x1_prev[...], gf_ref[...])
    h2_ref[...] = h2
    h2_hi = h2.astype(BF16)
    h2_lo = (h2 - h2_hi.astype(F32)).astype(BF16)
    logits = (jnp.dot(h2_hi, wrh_ref[...], preferred_element_type=F32)
              + jnp.dot(h2_lo, wrh_ref[...], preferred_element_type=F32)
              + jnp.dot(h2_hi, wrl_ref[...], preferred_element_type=F32)) + br_ref[...]
    ne = logits.shape[-1]
    eidx = lax.broadcasted_iota(jnp.int32, logits.shape, 1)
    kidx = lax.broadcasted_iota(jnp.int32, idx_ref.shape, 1)
    vals = jnp.zeros(gate_ref.shape, F32)
    idxs = jnp.zeros(idx_ref.shape, jnp.int32)
    cur = logits
    for k in range(TOP_K):
        m = jnp.max(cur, axis=-1, keepdims=True)
        sel = jnp.min(jnp.where(cur == m, eidx, ne), axis=-1, keepdims=True)
        vals = jnp.where(kidx == k, m, vals)
        idxs = jnp.where(kidx == k, sel, idxs)
        cur = jnp.where(eidx == sel, -jnp.inf, cur)
    e = jnp.exp(vals - vals[:, 0:1])
    gate_ref[...] = e / jnp.sum(e, axis=-1, keepdims=True)
    idx_ref[...] = idxs


    def read(k):
        attn_ref, hf_ref, hb_ref = groups[k]
        return (attn_ref[...].astype(F32),
                jnp.concatenate([hf_ref[0, n] + hb_ref[0, n] for n in range(nblk)], axis=-1))

    attn, h = read(0)
    for k in range(1, ng):
        attn_k, h_k = read(k)
        attn = jnp.where(i >= starts[k], attn_k, attn)
        h = jnp.where(i >= starts[k], h_k, h)
    an = _rms(attn, ga_ref[...])
    rn = _rms(h * _gelu_tanh(g_ref[...].astype(F32)), gl_ref[...])
    mixed = jnp.concatenate([an, rn], axis=-1).astype(BF16)
    x1 = x_ref[...] + jnp.dot(mixed, wo_ref[...], preferred_element_type=F32)
    x1_ref[...] = x1
    x1_prev[...] = x1


def _outproj(attns, hs, z, x, ga, gl, wo, gf, wr, br, *, g_blk, tm):
    n, d = x.shape
    aw = attns[0].shape[1]
    nblk = hs[0].shape[1]
    c = nblk * LRU_BLOCK_W
    ne = wr.shape[1]
    nt = n // tm
    wr_hi = wr.astype(BF16)
    wr_lo = (wr - wr_hi.astype(F32)).astype(BF16)
    const = lambda i: (0, 0)
    cur = lambda i: jnp.minimum(i, nt - 1)
    prv = lambda i: jnp.maximum(i - 1, 0)
    starts, group_specs, group_args = [], [], []
    s0 = 0
    for attn, h in zip(attns, hs):
        nb = attn.shape[0] // tm
        assert attn.shape[0] % tm == 0
        local = lambda i, s0=s0, nb=nb: jnp.clip(i - s0, 0, nb - 1)
        group_specs += [pl.BlockSpec((tm, aw), lambda i, f=local: (f(i), 0)),
                        pl.BlockSpec((1, nblk, tm, LRU_BLOCK_W), lambda i, f=local: (0, 0, f(i), 0)),
                        pl.BlockSpec((1, nblk, tm, LRU_BLOCK_W), lambda i, f=local: (1, 0, f(i), 0))]
        group_args += [attn, h, h]
        starts.append(s0)
        s0 += nb
    assert s0 == nt
    return pl.pallas_call(
        functools.partial(_outproj_kernel, starts=tuple(starts)),
        grid=(nt + 1,),
        in_specs=group_specs + [
                  pl.BlockSpec((tm, c), lambda i: (cur(i), g_blk)),
                  pl.BlockSpec((tm, d), lambda i: (cur(i), 0)),
                  pl.BlockSpec((1, aw), const),
                  pl.BlockSpec((1, c), const),
                  pl.BlockSpec((d, d), const),
                  pl.BlockSpec((1, d), const),
                  pl.BlockSpec((d, ne), const),
                  pl.BlockSpec((d, ne), const),
                  pl.BlockSpec((1, ne), const)],
        out_specs=[pl.BlockSpec((tm, d), lambda i: (cur(i), 0)),
                   pl.BlockSpec((tm, d), lambda i: (prv(i), 0)),
                   pl.BlockSpec((tm, TOP_K), lambda i: (prv(i), 0)),
                   pl.BlockSpec((tm, TOP_K), lambda i: (prv(i), 0))],
        out_shape=[jax.ShapeDtypeStruct((n, d), F32),
                   jax.ShapeDtypeStruct((n, d), F32),
                   jax.ShapeDtypeStruct((n, TOP_K), jnp.int32),
                   jax.ShapeDtypeStruct((n, TOP_K), F32)],
        scratch_shapes=[pltpu.VMEM((tm, d), F32)],
        compiler_params=_cparams("arbitrary"),
        name="outproj",
    )(*group_args, z, x, ga.reshape(1, aw), gl.reshape(1, c), wo, gf.reshape(1, d), wr_hi, wr_lo, br.reshape(1, ne))


def _moe_plan(top_idx, n_experts):
    n = top_idx.shape[0]
    m = n * TOP_K
    n_sb = -(-m // MOE_TM) + n_experts + 1
    flat_e = top_idx.reshape(-1)
    order = jnp.argsort(flat_e).astype(jnp.int32)
    counts = jnp.sum((flat_e[:, None] == jnp.arange(n_experts)[None, :]).astype(jnp.int32), axis=0)
    start = jnp.cumsum(counts) - counts
    sb_per_e = (counts + MOE_TM - 1) // MOE_TM
    sb_end = jnp.cumsum(sb_per_e)
    sb = jnp.arange(n_sb, dtype=jnp.int32)
    sb_e = jnp.minimum(jnp.sum((sb_end[None, :] <= sb[:, None]).astype(jnp.int32), axis=1), n_experts - 1)
    local = sb - (sb_end - sb_per_e)[sb_e]
    valid = jnp.clip(counts[sb_e] - local * MOE_TM, 0, MOE_TM)
    valid = jnp.where(sb < sb_end[-1], valid, 0).astype(jnp.int32)
    r = jnp.arange(MOE_TM, dtype=jnp.int32)
    live = r[None, :] < valid[:, None]
    sorted_pos = jnp.clip(start[sb_e][:, None] + local[:, None] * MOE_TM + r[None, :], 0, m - 1)
    a = order[sorted_pos]
    src = jnp.where(live, a // TOP_K, 0)
    dst = jnp.where(live, (a % TOP_K) * n + a // TOP_K, m + r[None, :])
    rows = jnp.concatenate([src, dst], axis=1).reshape(-1).astype(jnp.int32)
    return sb_e, valid, rows


def _wprep_kernel(w_ref, o_ref):
    grp = 2 * LANES
    r = lax.broadcasted_iota(jnp.int32, (grp, grp), 0)
    c = lax.broadcasted_iota(jnp.int32, (grp, grp), 1)
    perm = (r == jnp.where(c < LANES, 2 * c, 2 * (c - LANES) + 1)).astype(BF16)
    for g in range(w_ref.shape[3] // grp):
        w = w_ref[0, 0, :, g * grp:(g + 1) * grp].astype(BF16)
        o_ref[0, :, g * grp:(g + 1) * grp] = jnp.dot(w, perm, preferred_element_type=F32).astype(BF16)


def _wcast_kernel(w_ref, o_ref):
    o_ref[0] = w_ref[0, 0].astype(BF16)


def _wprep(w, layer, *, regroup, tr, tn):
    _, ne, r, c = w.shape
    return pl.pallas_call(
        _wprep_kernel if regroup else _wcast_kernel,
        grid=(ne, r // tr, c // tn),
        in_specs=[pl.BlockSpec((1, 1, tr, tn), lambda e, i, j: (layer, e, i, j))],
        out_specs=pl.BlockSpec((1, tr, tn), lambda e, i, j: (e, i, j)),
        out_shape=jax.ShapeDtypeStruct((ne, r, c), BF16),
        compiler_params=_cparams("arbitrary", "arbitrary", "arbitrary"),
        name="wprep" if regroup else "wcast",
    )(w)


def _regroup_gate_up(b):
    lead = b.shape[:-1]
    return b.reshape(lead + (-1, LANES, 2)).swapaxes(-1, -2).reshape(lead + (-1,))


def _moe_kernel(sbe_ref, live_ref, rows_hbm, h_hbm, wgu_ref, bgu_ref, wd_ref, bd_ref, y_hbm,
                rows_smem, xg0, xg1, acc0, acc1, sem_idx, sem_in, sem_out, *, nj):
    s = pl.program_id(0)
    j = pl.program_id(1)
    n_sb = pl.num_programs(0)
    xgs, accs = (xg0, xg1), (acc0, acc1)
    nsub = (live_ref[s] + MOE_SUB - 1) // MOE_SUB
    nsub_prev = jnp.where(s > 0, (live_ref[jnp.maximum(s - 1, 0)] + MOE_SUB - 1) // MOE_SUB, 0)
    grp_rows = MOE_TM // (MOE_SUBS_PER_BLOCK * nj)
    plan_words = 2 * MOE_TM

    def sub_rows(q, base=0):
        return pl.ds(pl.multiple_of(base + q * MOE_SUB, MOE_SUB), MOE_SUB)

    def start_rows(start_row, lo, hi):
        ngrp = (hi - lo) // MOE_ISSUE_UNROLL

        def group(g, c):
            for u in range(MOE_ISSUE_UNROLL):
                start_row(lo + g * MOE_ISSUE_UNROLL + u)
            return c

        def single(row, c):
            start_row(row)
            return c

        lax.fori_loop(0, ngrp, group, 0)
        lax.fori_loop(lo + ngrp * MOE_ISSUE_UNROLL, hi, single, 0)

    def wait_blocks(src, dst, sem, nblocks):
        def block(q, c):
            pltpu.make_async_copy(src.at[pl.ds(0, MOE_SUB), :], dst.at[pl.ds(0, MOE_SUB), :], sem).wait()
            return c

        lax.fori_loop(0, nblocks, block, 0)

    def idx_copy(sb, slot_of=None):
        slot = ((sb if slot_of is None else slot_of) % 4) * plan_words
        return pltpu.make_async_copy(rows_hbm.at[pl.ds(pl.multiple_of(sb * plan_words, plan_words), plan_words)],
                                     rows_smem.at[pl.ds(pl.multiple_of(slot, plan_words), plan_words)], sem_idx)

    def gather_row(sb, buf, sem, row):
        tok = rows_smem[(sb % 4) * plan_words + row]
        pltpu.make_async_copy(h_hbm.at[pl.ds(tok, 1), :], buf.at[pl.ds(row, 1), :], sem).start()

    def scatter_prev_row(buf, sem, row):
        dst = rows_smem[((s + 3) % 4) * plan_words + MOE_TM + row]
        pltpu.make_async_copy(buf.at[pl.ds(row, 1), :], y_hbm.at[pl.ds(dst, 1), :], sem).start()

    def remaining_groups(start_row, nsub_done):
        def per_j(jj, c):
            def per_q(q, c2):
                lo = (jj * MOE_SUBS_PER_BLOCK + q) * grp_rows
                start_rows(start_row, lo, lo + grp_rows)
                return c2

            lax.fori_loop(nsub_done, MOE_SUBS_PER_BLOCK, per_q, 0)
            return c

        lax.fori_loop(0, nj, per_j, 0)

    def run(p):
        xg_cur, xg_nxt = xgs[p], xgs[1 - p]
        acc_cur, acc_prv = accs[p], accs[1 - p]
        gather_cur = functools.partial(gather_row, s, xg_cur, sem_in.at[p])
        gather_nxt = functools.partial(gather_row, s + 1, xg_nxt, sem_in.at[1 - p])
        scatter_prv = functools.partial(scatter_prev_row, acc_prv, sem_out.at[1 - p])

        @pl.when(j == 0)
        def _prologue():
            if p == 0:
                @pl.when(s == 0)
                def _():
                    for sb in range(2):
                        idx_copy(sb).start()
                        idx_copy(sb).wait()
                    idx_copy(n_sb - 1, slot_of=3).start()
                    idx_copy(n_sb - 1, slot_of=3).wait()
                    acc_prv[...] = jnp.zeros_like(acc_prv)
                    fill = pltpu.make_async_copy(acc_prv, y_hbm.at[pl.ds(y_hbm.shape[0] - MOE_TM, MOE_TM), :],
                                                 sem_out.at[0])
                    fill.start()
                    fill.wait()

            @pl.when(jnp.logical_and(s > 0, s + 1 < n_sb))
            def _():
                idx_copy(s + 1).wait()

            @pl.when(s + 2 < n_sb)
            def _():
                idx_copy(s + 2).start()

            @pl.when(nsub > 0)
            def _():
                remaining_groups(gather_cur, nsub_prev)

            started = jnp.where(nsub > 0, MOE_SUBS_PER_BLOCK, nsub_prev)
            wait_blocks(h_hbm, xg_cur, sem_in.at[p], started)

            def init(q, c):
                acc_cur[sub_rows(q), :] = jnp.broadcast_to(bd_ref[0], (MOE_SUB, acc_cur.shape[1]))
                return c

            lax.fori_loop(0, MOE_SUBS_PER_BLOCK, init, 0)

        def compute(q, c):
            lo = pl.multiple_of((j * MOE_SUBS_PER_BLOCK + q) * grp_rows, grp_rows)
            for u in range(grp_rows):
                gather_nxt(lo + u)
            x = xg_cur[sub_rows(q), :].astype(BF16)
            hgu = jnp.dot(x, wgu_ref[0], preferred_element_type=F32) + bgu_ref[0]
            nchunk = hgu.shape[1] // LANES
            gate = jnp.concatenate([hgu[:, b * LANES:(b + 1) * LANES] for b in range(0, nchunk, 2)], axis=-1)
            up = jnp.concatenate([hgu[:, b * LANES:(b + 1) * LANES] for b in range(1, nchunk, 2)], axis=-1)
            gate = jnp.minimum(gate, SWIGLU_LIMIT)
            up = jnp.clip(up, -SWIGLU_LIMIT, SWIGLU_LIMIT)
            act = (up + 1.0) * (gate * jax.nn.sigmoid(SWIGLU_ALPHA * gate))
            for u in range(grp_rows):
                scatter_prv(lo + u)
            acc_cur[sub_rows(q), :] += jnp.dot(act.astype(BF16), wd_ref[0], preferred_element_type=F32)
            return c

        lax.fori_loop(0, nsub, compute, 0)

        @pl.when(j == nj - 1)
        def _epilogue():
            @pl.when(nsub_prev > 0)
            def _():
                remaining_groups(scatter_prv, nsub)

            started = jnp.where(nsub_prev > 0, MOE_SUBS_PER_BLOCK, nsub)
            wait_blocks(acc_prv, y_hbm, sem_out.at[1 - p], started)

    for p in range(2):
        pl.when(s % 2 == p)(functools.partial(run, p))


def _moe(h2, sb_e, live, rows, wgu, bgu, wdn, bdn):
    n, d = h2.shape
    ne, f = wdn.shape[0], wdn.shape[1]
    nj = f // MOE_TF
    n_sb = sb_e.shape[0]

    def ff(s, j, live_ref):
        return jnp.where(live_ref[s] > 0, j, nj - 1)

    grid_spec = pltpu.PrefetchScalarGridSpec(
        num_scalar_prefetch=2,
        grid=(n_sb, nj),
        in_specs=[pl.BlockSpec(memory_space=pl.ANY),
                  pl.BlockSpec(memory_space=pl.ANY),
                  pl.BlockSpec((1, d, 2 * MOE_TF), lambda s, j, e, ns: (e[s], 0, ff(s, j, ns))),
                  pl.BlockSpec((1, 1, 2 * MOE_TF), lambda s, j, e, ns: (e[s], 0, ff(s, j, ns))),
                  pl.BlockSpec((1, MOE_TF, d), lambda s, j, e, ns: (e[s], ff(s, j, ns), 0)),
                  pl.BlockSpec((1, 1, d), lambda s, j, e, ns: (e[s], 0, 0))],
        out_specs=pl.BlockSpec(memory_space=pl.ANY),
        scratch_shapes=[pltpu.SMEM((4 * 2 * MOE_TM,), jnp.int32),
                        pltpu.VMEM((MOE_TM, d), F32),
                        pltpu.VMEM((MOE_TM, d), F32),
                        pltpu.VMEM((MOE_TM, d), F32),
                        pltpu.VMEM((MOE_TM, d), F32),
                        pltpu.SemaphoreType.DMA(()),
                        pltpu.SemaphoreType.DMA((2,)),
                        pltpu.SemaphoreType.DMA((2,))])
    assert MOE_TM % (MOE_SUBS_PER_BLOCK * nj) == 0 and n_sb >= 3
    return pl.pallas_call(
        functools.partial(_moe_kernel, nj=nj),
        grid_spec=grid_spec,
        out_shape=jax.ShapeDtypeStruct((TOP_K * n + MOE_TM, d), F32),
        compiler_params=_cparams("arbitrary", "arbitrary"),
        name="moe",
    )(sb_e, live, rows, h2, wgu, bgu, wdn, bdn)


def _combine_kernel(x_ref, y0_ref, y1_ref, y2_ref, y3_ref, gate_ref, gfin_ref, *o_refs, final, starts):
    g = gate_ref[...]
    x = x_ref[...]
    for k, y_ref in enumerate((y0_ref, y1_ref, y2_ref, y3_ref)):
        x = x + g[:, k:k + 1] * y_ref[...]
    x = _rms(x, gfin_ref[...]) if final else x
    if len(o_refs) == 1:
        o_refs[0][...] = x
        return
    i = pl.program_id(0)
    bounds = list(starts) + [pl.num_programs(0)]
    for k, o_ref in enumerate(o_refs):
        @pl.when(jnp.logical_and(i >= bounds[k], i < bounds[k + 1]))
        def _(o_ref=o_ref):
            o_ref[...] = x


def _combine(x1, y, gates, gfin, *, final, tm, split=None):
    n, d = x1.shape
    nb = n // tm
    y_spec = lambda k: pl.BlockSpec((tm, d), lambda i: (k * nb + i, 0))
    if split is None:
        starts = (0,)
        out_specs = pl.BlockSpec((tm, d), lambda i: (i, 0))
        out_shape = jax.ShapeDtypeStruct((n, d), F32)
    else:
        starts, out_specs, out_shape, s0 = [], [], [], 0
        for nk in split:
            assert nk % tm == 0
            out_specs.append(pl.BlockSpec((tm, d), lambda i, s0=s0, nbk=nk // tm: (jnp.clip(i - s0, 0, nbk - 1), 0)))
            out_shape.append(jax.ShapeDtypeStruct((nk, d), F32))
            starts.append(s0)
            s0 += nk // tm
        starts = tuple(starts)
    return pl.pallas_call(
        functools.partial(_combine_kernel, final=final, starts=starts),
        grid=(nb,),
        in_specs=[pl.BlockSpec((tm, d), lambda i: (i, 0)), y_spec(0), y_spec(1), y_spec(2), y_spec(3),
                  pl.BlockSpec((tm, TOP_K), lambda i: (i, 0)),
                  pl.BlockSpec((1, d), lambda i: (0, 0))],
        out_specs=out_specs,
        out_shape=out_shape,
        compiler_params=_cparams("arbitrary"),
        name="combine",
    )(x1, y, y, y, y, gates, gfin.reshape(1, d))


def _pick(n, *cands):
    for c in cands:
        if n % c == 0:
            return c
    raise ValueError(f"no tile for {n}")


def kernel(x_prompt, x_sample, g_mix, w_in, rpb, conv_w, conv_b, w_rg_a, b_rg_a, w_rg_x, b_rg_x, rg_lambda,
           g_attn_out, g_lru_out, w_out, g_ffn, w_router, b_router, w_gate_up, b_gate_up, w_down, b_down, g_final):
    depth = w_in.shape[0]
    d = x_prompt.shape[-1]
    attn_w = rpb.shape[1] * HEAD_DIM
    lru_w = conv_w.shape[-1]
    ne = w_router.shape[-1]
    f = w_down.shape[2]
    seqs = []
    off = 0
    for xs in (x_prompt, x_sample):
        seqs.append((off, xs.shape[0], xs.shape[1]))
        off += xs.shape[0] * xs.shape[1]
    n = off
    x = jnp.concatenate([x_prompt.reshape(-1, d), x_sample.reshape(-1, d)], axis=0)
    assert attn_w % LANES == 0 and lru_w == attn_w, "column blocks of z assume equal attention / recurrent widths"

    for l in range(depth):
        z = _inproj(x, g_mix[l], w_in[l].astype(BF16), tm=_pick(n, 1024, 512, 256), tn=_pick(w_in.shape[2], 1024, 512, 256))
        bias = _na_bias_table(rpb[l])
        wa, wx = w_rg_a[l].astype(BF16), w_rg_x[l].astype(BF16)
        attn, h = [], []
        for (o, b, t) in seqs:
            attn.append(_na(z, bias, row_off=o, batch=b, t=t, attn_w=attn_w, tq=_pick(t, 512, 256, 128, 64)))
            lru_args = (z, conv_w[l], conv_b[l], wa, b_rg_a[l], wx, b_rg_x[l], rg_lambda[l])
            if b == SUBLANES and n % t == 0 and o % (b * t) == 0:
                h.append(_lru_batched(*lru_args, row_off=o, t=t, col_blk=3, tt=_pick(t, 128, 64)))
            else:
                h.append(_lru(*lru_args, row_off=o, batch=b, t=t, col_blk=3, tt=_pick(t, 512, 256, 128, 64)))
        x1, h2, top_idx, gates = _outproj(attn, h, z, x, g_attn_out[l], g_lru_out[l], w_out[l].astype(BF16),
                                          g_ffn[l], w_router[l], b_router[l], g_blk=4, tm=_pick(n, 256, 128))
        sb_e, live, rows = _moe_plan(top_idx, ne)
        wgu = _wprep(w_gate_up, l, regroup=True, tr=d, tn=_pick(2 * f, 1024, 512, 256))
        wdn = _wprep(w_down, l, regroup=False, tr=_pick(f, 512, 256), tn=d)
        bgu = _regroup_gate_up(b_gate_up[l]).reshape(ne, 1, 2 * f)
        y = _moe(h2, sb_e, live, rows, wgu, bgu, wdn, b_down[l].reshape(ne, 1, d))
        last = l == depth - 1
        x = _combine(x1, y, gates, g_final, final=last, tm=_pick(n, 256, 128),
                     split=tuple(b * t for (_, b, t) in seqs) if last else None)

    return x[0].reshape(x_prompt.shape), x[1].reshape(x_sample.shape)
```

```python
import functools

import jax
import jax.numpy as jnp
from jax import lax
from jax.experimental import pallas as pl
from jax.experimental.pallas import tpu as pltpu

F32 = jnp.float32
BF16 = jnp.bfloat16

GRID_W = 64
HEAD_DIM = 64
NA_ROWS = 8
NA_COLS = 16
CONV_W = 4
RG_C = 8.0
LRU_BLOCK_W = 128
TOP_K = 4
SWIGLU_LIMIT = 7.0
SWIGLU_ALPHA = 1.702
NORM_EPS = 1e-5
MASK_BIAS = -1e30

LANES = 128
SUBLANES = 8
BF16_SUBLANES = 16
VMEM_LIMIT = 56 * 1024 * 1024

MOE_SUB = 256
MOE_SUBS_PER_BLOCK = 4
MOE_TM = MOE_SUB * MOE_SUBS_PER_BLOCK
MOE_TF = 512
MOE_ISSUE_UNROLL = 4


def _cparams(*sem):
    return pltpu.CompilerParams(dimension_semantics=sem, vmem_limit_bytes=VMEM_LIMIT)


def _rms(x, g):
    return x * lax.rsqrt(jnp.mean(x * x, axis=-1, keepdims=True) + NORM_EPS) * g


def _inproj_kernel(x_ref, g_ref, w_ref, o_ref, xn_ref):
    @pl.when(pl.program_id(1) == 0)
    def _():
        xn_ref[...] = _rms(x_ref[...], g_ref[...]).astype(BF16)

    o_ref[...] = jnp.dot(xn_ref[...], w_ref[...], preferred_element_type=F32).astype(o_ref.dtype)


def _inproj(x, g, w, *, tm, tn):
    n, d = x.shape
    wn = w.shape[1]
    return pl.pallas_call(
        _inproj_kernel,
        grid=(n // tm, wn // tn),
        in_specs=[pl.BlockSpec((tm, d), lambda i, j: (i, 0)),
                  pl.BlockSpec((1, d), lambda i, j: (0, 0)),
                  pl.BlockSpec((d, tn), lambda i, j: (0, j))],
        out_specs=pl.BlockSpec((tm, tn), lambda i, j: (i, j)),
        out_shape=jax.ShapeDtypeStruct((n, wn), BF16),
        scratch_shapes=[pltpu.VMEM((tm, d), BF16)],
        compiler_params=_cparams("arbitrary", "arbitrary"),
        name="inproj",
    )(x, g.reshape(1, d), w)


def _na_bias_table(rpb):
    h = rpb.shape[0]
    c = jnp.arange(GRID_W)
    col_start = jnp.clip(c - NA_COLS // 2, 0, GRID_W - NA_COLS)
    j = jnp.arange(GRID_W)
    inwin = (j[None, :] >= col_start[:, None]) & (j[None, :] < col_start[:, None] + NA_COLS)
    dc = jnp.clip(j[None, :] - c[:, None] + (NA_COLS - 1), 0, 2 * NA_COLS - 2)
    delta = jnp.arange(NA_ROWS)
    kk = jnp.arange(NA_ROWS)
    dr = kk[None, :] - delta[:, None] + (NA_ROWS - 1)
    row_sel = jax.nn.one_hot(dr, 2 * NA_ROWS - 1, dtype=F32)
    col_sel = jax.nn.one_hot(dc, 2 * NA_COLS - 1, dtype=F32)
    t = jnp.einsum("dka,hab,cjb->hdkcj", row_sel, rpb.astype(F32), col_sel, precision=lax.Precision.HIGHEST)
    t = jnp.where(inwin[None, None, None], t, MASK_BIAS)
    t = t.transpose(0, 1, 3, 2, 4).reshape(h // 2, 2, NA_ROWS, GRID_W, NA_ROWS * GRID_W)
    return t.transpose(0, 2, 1, 3, 4).reshape(h // 2, NA_ROWS, 2 * GRID_W, NA_ROWS * GRID_W)


def _na_kernel(q_ref, k_ref, v_ref, b_ref, o_ref, s_scr, p_scr, *, rows, rows_per_tile):
    ti = pl.program_id(2)
    lane = lax.broadcasted_iota(jnp.int32, (GRID_W, 2 * HEAD_DIM), 1)
    first_head = lane < HEAD_DIM
    win = NA_ROWS * GRID_W

    def window_start(rr):
        r = ti * rows_per_tile + rr
        r0 = jnp.clip(r - NA_ROWS // 2, 0, rows - NA_ROWS)
        return r - r0, pl.multiple_of(r0 * GRID_W, GRID_W)

    for rr in range(rows_per_tile):
        delta, ks = window_start(rr)
        q = q_ref[pl.ds(rr * GRID_W, GRID_W), :] * (HEAD_DIM ** -0.5)
        zero = jnp.zeros_like(q)
        qm = jnp.concatenate([jnp.where(first_head, q, zero), jnp.where(first_head, zero, q)], axis=0)
        s = lax.dot_general(qm, k_ref[pl.ds(ks, win), :], (((1,), (1,)), ((), ())), preferred_element_type=F32)
        s_scr[rr] = s + b_ref[0, delta]

    for rr in range(rows_per_tile):
        s = s_scr[rr]
        e = jnp.exp(s - jnp.max(s, axis=-1, keepdims=True))
        p_scr[rr] = (e * (1.0 / jnp.sum(e, axis=-1, keepdims=True))).astype(BF16)

    for rr in range(rows_per_tile):
        _, ks = window_start(rr)
        o = jnp.dot(p_scr[rr], v_ref[pl.ds(ks, win), :], preferred_element_type=F32)
        out = jnp.where(first_head, o[:GRID_W], o[GRID_W:])
        o_ref[pl.ds(rr * GRID_W, GRID_W), :] = out.astype(o_ref.dtype)


def _na(z, bias, *, row_off, batch, t, attn_w, tq):
    rows = t // GRID_W
    assert rows >= NA_ROWS and t % tq == 0 and row_off % t == 0 and tq % GRID_W == 0
    hp = attn_w // LANES
    nq = t // tq
    qoff, koff = row_off // tq, row_off // t
    kern = functools.partial(_na_kernel, rows=rows, rows_per_tile=tq // GRID_W)
    return pl.pallas_call(
        kern,
        grid=(batch, hp, nq),
        in_specs=[pl.BlockSpec((tq, LANES), lambda b, h, i: (qoff + b * nq + i, h)),
                  pl.BlockSpec((t, LANES), lambda b, h, i: (koff + b, hp + h)),
                  pl.BlockSpec((t, LANES), lambda b, h, i: (koff + b, 2 * hp + h)),
                  pl.BlockSpec((1,) + bias.shape[1:], lambda b, h, i: (h, 0, 0, 0))],
        out_specs=pl.BlockSpec((tq, LANES), lambda b, h, i: (b * nq + i, h)),
        out_shape=jax.ShapeDtypeStruct((batch * t, attn_w), BF16),
        scratch_shapes=[pltpu.VMEM((tq // GRID_W, 2 * GRID_W, NA_ROWS * GRID_W), F32),
                        pltpu.VMEM((tq // GRID_W, 2 * GRID_W, NA_ROWS * GRID_W), BF16)],
        compiler_params=_cparams("arbitrary", "arbitrary", "arbitrary"),
        name="na",
    )(z, z, z, bias)


def _softplus(x):
    return jnp.maximum(x, 0.0) + jnp.log(1.0 + jnp.exp(-jnp.abs(x)))


def _sigmoid(x):
    return 0.5 * jnp.tanh(0.5 * x) + 0.5


def _lru_coeffs(u, prev, nxt, ext, refs, *, d, ci, tt, nt, t, store):
    cw_ref, cb_ref, wa_ref, ba_ref, wx_ref, bx_ref, lam_ref = refs
    halo = SUBLANES
    ext[pl.ds(0, halo), :] = jnp.where(ci == 0, 0.0, prev)
    ext[pl.ds(halo, tt), :] = u
    ext[pl.ds(halo + tt, halo), :] = jnp.where(ci == nt - 1, 0.0, nxt)
    c = cb_ref[...]
    for j in range(CONV_W):
        c = c + cw_ref[pl.ds(j, 1), :] * ext[pl.ds(halo - CONV_W // 2 + j, tt), :]
    cbf = c.astype(BF16)

    tpos = ci * tt + lax.broadcasted_iota(jnp.int32, (tt, 1), 0)
    is_first = tpos == jnp.where(d == 0, 0, t - 1)
    sp = _softplus(-lam_ref[0])
    for n in range(wa_ref.shape[1]):
        sl = slice(n * LRU_BLOCK_W, (n + 1) * LRU_BLOCK_W)
        cn = cbf[:, sl]
        r = _sigmoid(jnp.dot(cn, wa_ref[0, n], preferred_element_type=F32) + ba_ref[0, :, sl])
        ig = _sigmoid(jnp.dot(cn, wx_ref[0, n], preferred_element_type=F32) + bx_ref[0, :, sl])
        a = jnp.exp(-RG_C * r * sp[:, sl])
        mult = jnp.where(is_first, 1.0, jnp.sqrt(1.0 - a * a))
        store(n, a, mult * (ig * c[:, sl]))


def _lru_batched_kernel(u_ref, up_ref, un_ref, cw_ref, cb_ref, wa_ref, ba_ref, wx_ref, bx_ref, lam_ref, o_ref,
                        ext, a_scr, b_scr, carry, *, tt, nt, t):
    d = pl.program_id(0)
    i = pl.program_id(1)
    ci = jnp.where(d == 0, i, nt - 1 - i)
    nblk = a_scr.shape[0]
    refs = (cw_ref, cb_ref, wa_ref, ba_ref, wx_ref, bx_ref, lam_ref)

    def per_sequence(b, c0):
        def store(n, a, bb):
            a_scr[n, :, b, :] = a
            b_scr[n, :, b, :] = bb

        _lru_coeffs(u_ref[b].astype(F32), up_ref[b].astype(F32)[BF16_SUBLANES - SUBLANES:, :],
                    un_ref[b].astype(F32)[:SUBLANES, :], ext, refs, d=d, ci=ci, tt=tt, nt=nt, t=t, store=store)
        return c0

    lax.fori_loop(0, SUBLANES, per_sequence, 0)

    @pl.when(i == 0)
    def _():
        carry[...] = jnp.zeros_like(carry)

    def step(jj, hs):
        j = jnp.where(d == 0, jj, tt - 1 - jj)
        new = []
        for n in range(nblk):
            h = a_scr[n, j] * hs[n] + b_scr[n, j]
            b_scr[n, j] = h
            new.append(h)
        return tuple(new)

    hs = lax.fori_loop(0, tt, step, tuple(carry[n] for n in range(nblk)))
    for n in range(nblk):
        carry[n] = hs[n]

    def write(b, c0):
        for n in range(nblk):
            o_ref[0, n, b] = b_scr[n, :, b, :]
        return c0

    lax.fori_loop(0, SUBLANES, write, 0)


def _lru_batched(z, cw, cb, wa, ba, wx, bx, lam, *, row_off, t, col_blk, tt):
    c = cw.shape[1]
    nblk = c // LRU_BLOCK_W
    nt = t // tt
    ntot = z.shape[0]
    nb = SUBLANES
    assert t % tt == 0 and ntot % t == 0 and row_off % (nb * t) == 0 and tt % BF16_SUBLANES == 0
    z3 = z.reshape(ntot // t, t, z.shape[1])
    g = row_off // (nb * t)
    hb = BF16_SUBLANES

    def chunk(dd, i):
        return jnp.where(dd == 0, i, nt - 1 - i)

    kern = functools.partial(_lru_batched_kernel, tt=tt, nt=nt, t=t)
    dir_w = lambda dd, i: (dd, 0, 0, 0)
    dir_v = lambda dd, i: (dd, 0, 0)
    out = pl.pallas_call(
        kern,
        grid=(2, nt),
        in_specs=[pl.BlockSpec((nb, tt, c), lambda dd, i: (g, chunk(dd, i), col_blk)),
                  pl.BlockSpec((nb, hb, c), lambda dd, i: (g, jnp.maximum(chunk(dd, i) * (tt // hb) - 1, 0), col_blk)),
                  pl.BlockSpec((nb, hb, c), lambda dd, i: (g, jnp.minimum((chunk(dd, i) + 1) * (tt // hb), t // hb - 1), col_blk)),
                  pl.BlockSpec((CONV_W, c), lambda dd, i: (0, 0)),
                  pl.BlockSpec((1, c), lambda dd, i: (0, 0)),
                  pl.BlockSpec((1, nblk, LRU_BLOCK_W, LRU_BLOCK_W), dir_w),
                  pl.BlockSpec((1, 1, c), dir_v),
                  pl.BlockSpec((1, nblk, LRU_BLOCK_W, LRU_BLOCK_W), dir_w),
                  pl.BlockSpec((1, 1, c), dir_v),
                  pl.BlockSpec((1, 1, c), dir_v)],
        out_specs=pl.BlockSpec((1, nblk, nb, tt, LRU_BLOCK_W), lambda dd, i: (dd, 0, 0, chunk(dd, i), 0)),
        out_shape=jax.ShapeDtypeStruct((2, nblk, nb, t, LRU_BLOCK_W), F32),
        scratch_shapes=[pltpu.VMEM((tt + 2 * SUBLANES, c), F32),
                        pltpu.VMEM((nblk, tt, nb, LRU_BLOCK_W), F32),
                        pltpu.VMEM((nblk, tt, nb, LRU_BLOCK_W), F32),
                        pltpu.VMEM((nblk, nb, LRU_BLOCK_W), F32)],
        compiler_params=_cparams("arbitrary", "arbitrary"),
        name="lru_batched",
    )(z3, z3, z3, cw, cb.reshape(1, c), wa, ba.reshape(2, 1, c), wx, bx.reshape(2, 1, c), lam.reshape(2, 1, c))
    return out.reshape(2, nblk, nb * t, LRU_BLOCK_W)


def _lru_kernel(u_ref, up_ref, un_ref, cw_ref, cb_ref, wa_ref, ba_ref, wx_ref, bx_ref, lam_ref, o_ref,
                ext, a_scr, b_scr, carry, *, tt, nt, t):
    d = pl.program_id(1)
    i = pl.program_id(2)
    ci = jnp.where(d == 0, i, nt - 1 - i)
    nblk = a_scr.shape[0]
    seg = tt // SUBLANES

    def store(n, a, bb):
        a_scr[n] = a
        b_scr[n] = bb

    _lru_coeffs(u_ref[...].astype(F32), up_ref[...].astype(F32)[BF16_SUBLANES - SUBLANES:, :],
                un_ref[...].astype(F32)[:SUBLANES, :], ext,
                (cw_ref, cb_ref, wa_ref, ba_ref, wx_ref, bx_ref, lam_ref), d=d, ci=ci, tt=tt, nt=nt, t=t, store=store)

    @pl.when(i == 0)
    def _():
        carry[...] = jnp.zeros_like(carry)

    def step_index(jj):
        return jnp.where(d == 0, jj, seg - 1 - jj)

    def load(ref, n, j):
        return ref[n, pl.ds(j, SUBLANES, stride=seg), :]

    def local_step(jj, hp):
        j = step_index(jj)
        hs, ps = hp
        new_h, new_p = [], []
        for n in range(nblk):
            a = load(a_scr, n, j)
            new_h.append(a * hs[n] + load(b_scr, n, j))
            new_p.append(a * ps[n])
        return tuple(new_h), tuple(new_p)

    zeros = tuple(jnp.zeros((SUBLANES, LRU_BLOCK_W), F32) for _ in range(nblk))
    ones = tuple(jnp.ones((SUBLANES, LRU_BLOCK_W), F32) for _ in range(nblk))
    h_end, p_end = lax.fori_loop(0, seg, local_step, (zeros, ones))

    cin, cout = [], []
    for n in range(nblk):
        c0 = carry[pl.ds(n, 1), :]
        fwd, cur = [], c0
        for s in range(SUBLANES):
            fwd.append(cur)
            cur = h_end[n][s:s + 1] + p_end[n][s:s + 1] * cur
        fwd_out = cur
        bwd, cur = [None] * SUBLANES, c0
        for s in reversed(range(SUBLANES)):
            bwd[s] = cur
            cur = h_end[n][s:s + 1] + p_end[n][s:s + 1] * cur
        bwd_out = cur
        cin.append(jnp.where(d == 0, jnp.concatenate(fwd, axis=0), jnp.concatenate(bwd, axis=0)))
        cout.append(jnp.where(d == 0, fwd_out, bwd_out))

    def final_step(jj, hs):
        j = step_index(jj)
        new_h = []
        for n in range(nblk):
            h = load(a_scr, n, j) * hs[n] + load(b_scr, n, j)
            o_ref[0, n, pl.ds(j, SUBLANES, stride=seg), :] = h
            new_h.append(h)
        return tuple(new_h)

    lax.fori_loop(0, seg, final_step, tuple(cin))
    for n in range(nblk):
        carry[pl.ds(n, 1), :] = cout[n]


def _lru(z, cw, cb, wa, ba, wx, bx, lam, *, row_off, batch, t, col_blk, tt):
    c = cw.shape[1]
    nblk = c // LRU_BLOCK_W
    nt = t // tt
    ntot = z.shape[0]
    assert t % tt == 0 and row_off % tt == 0 and tt % (SUBLANES * SUBLANES) == 0
    hb = BF16_SUBLANES
    base = row_off // tt

    def chunk(dd, i):
        return jnp.where(dd == 0, i, nt - 1 - i)

    def u_map(b, dd, i):
        return (base + b * nt + chunk(dd, i), col_blk)

    def prev_map(b, dd, i):
        return (jnp.maximum((row_off + b * t + chunk(dd, i) * tt) // hb - 1, 0), col_blk)

    def next_map(b, dd, i):
        return (jnp.minimum((row_off + b * t + (chunk(dd, i) + 1) * tt) // hb, ntot // hb - 1), col_blk)

    kern = functools.partial(_lru_kernel, tt=tt, nt=nt, t=t)
    dir_w = lambda b, dd, i: (dd, 0, 0, 0)
    dir_v = lambda b, dd, i: (dd, 0, 0)
    return pl.pallas_call(
        kern,
        grid=(batch, 2, nt),
        in_specs=[pl.BlockSpec((tt, c), u_map),
                  pl.BlockSpec((hb, c), prev_map),
                  pl.BlockSpec((hb, c), next_map),
                  pl.BlockSpec((CONV_W, c), lambda b, dd, i: (0, 0)),
                  pl.BlockSpec((1, c), lambda b, dd, i: (0, 0)),
                  pl.BlockSpec((1, nblk, LRU_BLOCK_W, LRU_BLOCK_W), dir_w),
                  pl.BlockSpec((1, 1, c), dir_v),
                  pl.BlockSpec((1, nblk, LRU_BLOCK_W, LRU_BLOCK_W), dir_w),
                  pl.BlockSpec((1, 1, c), dir_v),
                  pl.BlockSpec((1, 1, c), dir_v)],
        out_specs=pl.BlockSpec((1, nblk, tt, LRU_BLOCK_W), lambda b, dd, i: (dd, 0, b * nt + chunk(dd, i), 0)),
        out_shape=jax.ShapeDtypeStruct((2, nblk, batch * t, LRU_BLOCK_W), F32),
        scratch_shapes=[pltpu.VMEM((tt + 2 * SUBLANES, c), F32),
                        pltpu.VMEM((nblk, tt, LRU_BLOCK_W), F32),
                        pltpu.VMEM((nblk, tt, LRU_BLOCK_W), F32),
                        pltpu.VMEM((nblk, LRU_BLOCK_W), F32)],
        compiler_params=_cparams("arbitrary", "arbitrary", "arbitrary"),
        name="lru",
    )(z, z, z, cw, cb.reshape(1, c), wa, ba.reshape(2, 1, c), wx, bx.reshape(2, 1, c), lam.reshape(2, 1, c))


def _gelu_tanh(x):
    return 0.5 * x * (1.0 + jnp.tanh(0.7978845608028654 * (x + 0.044715 * (x * x * x))))


def _outproj_kernel(*refs, starts):
    ng = len(starts)
    groups = [refs[3 * k:3 * k + 3] for k in range(ng)]
    (g_ref, x_ref, ga_ref, gl_ref, wo_ref, gf_ref, wrh_ref, wrl_ref, br_ref,
     x1_ref, h2_ref, idx_ref, gate_ref, x1_prev) = refs[3 * ng:]
    i = pl.program_id(0)
    nblk = groups[0][1].shape[1]

    @pl.when(i == 0)
    def _():
        x1_prev[...] = jnp.zeros_like(x1_prev)

    h2 = _rms(x1_prev[...], gf_ref[...])
    h2_ref[...] = h2
    h2_hi = h2.astype(BF16)
    h2_lo = (h2 - h2_hi.astype(F32)).astype(BF16)
    logits = (jnp.dot(h2_hi, wrh_ref[...], preferred_element_type=F32)
              + jnp.dot(h2_lo, wrh_ref[...], preferred_element_type=F32)
              + jnp.dot(h2_hi, wrl_ref[...], preferred_element_type=F32)) + br_ref[...]
    ne = logits.shape[-1]
    eidx = lax.broadcasted_iota(jnp.int32, logits.shape, 1)
    kidx = lax.broadcasted_iota(jnp.int32, idx_ref.shape, 1)
    vals = jnp.zeros(gate_ref.shape, F32)
    idxs = jnp.zeros(idx_ref.shape, jnp.int32)
    cur = logits
    for k in range(TOP_K):
        m = jnp.max(cur, axis=-1, keepdims=True)
        sel = jnp.min(jnp.where(cur == m, eidx, ne), axis=-1, keepdims=True)
        vals = jnp.where(kidx == k, m, vals)
        idxs = jnp.where(kidx == k, sel, idxs)
        cur = jnp.where(eidx == sel, -jnp.inf, cur)
    e = jnp.exp(vals - vals[:, 0:1])
    gate_ref[...] = e / jnp.sum(e, axis=-1, keepdims=True)
    idx_ref[...] = idxs


    def read(k):
        attn_ref, hf_ref, hb_ref = groups[k]
        return (attn_ref[...].astype(F32),
                jnp.concatenate([hf_ref[0, n] + hb_ref[0, n] for n in range(nblk)], axis=-1))

    attn, h = read(0)
    for k in range(1, ng):
        attn_k, h_k = read(k)
        attn = jnp.where(i >= starts[k], attn_k, attn)
        h = jnp.where(i >= starts[k], h_k, h)
    an = _rms(attn, ga_ref[...])
    rn = _rms(h * _gelu_tanh(g_ref[...].astype(F32)), gl_ref[...])
    mixed = jnp.concatenate([an, rn], axis=-1).astype(BF16)
    x1 = x_ref[...] + jnp.dot(mixed, wo_ref[...], preferred_element_type=F32)
    x1_ref[...] = x1
    x1_prev[...] = x1


def _outproj(attns, hs, z, x, ga, gl, wo, gf, wr, br, *, g_blk, tm):
    n, d = x.shape
    aw = attns[0].shape[1]
    nblk = hs[0].shape[1]
    c = nblk * LRU_BLOCK_W
    ne = wr.shape[1]
    nt = n // tm
    wr_hi = wr.astype(BF16)
    wr_lo = (wr - wr_hi.astype(F32)).astype(BF16)
    const = lambda i: (0, 0)
    cur = lambda i: jnp.minimum(i, nt - 1)
    prv = lambda i: jnp.maximum(i - 1, 0)
    starts, group_specs, group_args = [], [], []
    s0 = 0
    for attn, h in zip(attns, hs):
        nb = attn.shape[0] // tm
        assert attn.shape[0] % tm == 0
        local = lambda i, s0=s0, nb=nb: jnp.clip(i - s0, 0, nb - 1)
        group_specs += [pl.BlockSpec((tm, aw), lambda i, f=local: (f(i), 0)),
                        pl.BlockSpec((1, nblk, tm, LRU_BLOCK_W), lambda i, f=local: (0, 0, f(i), 0)),
                        pl.BlockSpec((1, nblk, tm, LRU_BLOCK_W), lambda i, f=local: (1, 0, f(i), 0))]
        group_args += [attn, h, h]
        starts.append(s0)
        s0 += nb
    assert s0 == nt
    return pl.pallas_call(
        functools.partial(_outproj_kernel, starts=tuple(starts)),
        grid=(nt + 1,),
        in_specs=group_specs + [
                  pl.BlockSpec((tm, c), lambda i: (cur(i), g_blk)),
                  pl.BlockSpec((tm, d), lambda i: (cur(i), 0)),
                  pl.BlockSpec((1, aw), const),
                  pl.BlockSpec((1, c), const),
                  pl.BlockSpec((d, d), const),
                  pl.BlockSpec((1, d), const),
                  pl.BlockSpec((d, ne), const),
                  pl.BlockSpec((d, ne), const),
                  pl.BlockSpec((1, ne), const)],
        out_specs=[pl.BlockSpec((tm, d), lambda i: (cur(i), 0)),
                   pl.BlockSpec((tm, d), lambda i: (prv(i), 0)),
                   pl.BlockSpec((tm, TOP_K), lambda i: (prv(i), 0)),
                   pl.BlockSpec((tm, TOP_K), lambda i: (prv(i), 0))],
        out_shape=[jax.ShapeDtypeStruct((n, d), F32),
                   jax.ShapeDtypeStruct((n, d), F32),
                   jax.ShapeDtypeStruct((n, TOP_K), jnp.int32),
                   jax.ShapeDtypeStruct((n, TOP_K), F32)],
        scratch_shapes=[pltpu.VMEM((tm, d), F32)],
        compiler_params=_cparams("arbitrary"),
        name="outproj",
    )(*group_args, z, x, ga.reshape(1, aw), gl.reshape(1, c), wo, gf.reshape(1, d), wr_hi, wr_lo, br.reshape(1, ne))


def _moe_plan(top_idx, n_experts):
    n = top_idx.shape[0]
    m = n * TOP_K
    n_sb = -(-m // MOE_TM) + n_experts + 1
    flat_e = top_idx.reshape(-1)
    order = jnp.argsort(flat_e).astype(jnp.int32)
    counts = jnp.sum((flat_e[:, None] == jnp.arange(n_experts)[None, :]).astype(jnp.int32), axis=0)
    start = jnp.cumsum(counts) - counts
    sb_per_e = (counts + MOE_TM - 1) // MOE_TM
    sb_end = jnp.cumsum(sb_per_e)
    sb = jnp.arange(n_sb, dtype=jnp.int32)
    sb_e = jnp.minimum(jnp.sum((sb_end[None, :] <= sb[:, None]).astype(jnp.int32), axis=1), n_experts - 1)
    local = sb - (sb_end - sb_per_e)[sb_e]
    valid = jnp.clip(counts[sb_e] - local * MOE_TM, 0, MOE_TM)
    valid = jnp.where(sb < sb_end[-1], valid, 0).astype(jnp.int32)
    r = jnp.arange(MOE_TM, dtype=jnp.int32)
    live = r[None, :] < valid[:, None]
    sorted_pos = jnp.clip(start[sb_e][:, None] + local[:, None] * MOE_TM + r[None, :], 0, m - 1)
    a = order[sorted_pos]
    src = jnp.where(live, a // TOP_K, 0)
    dst = jnp.where(live, (a % TOP_K) * n + a // TOP_K, m + r[None, :])
    rows = jnp.concatenate([src, dst], axis=1).reshape(-1).astype(jnp.int32)
    return sb_e, valid, rows


def _wprep_kernel(w_ref, o_ref):
    grp = 2 * LANES
    r = lax.broadcasted_iota(jnp.int32, (grp, grp), 0)
    c = lax.broadcasted_iota(jnp.int32, (grp, grp), 1)
    perm = (r == jnp.where(c < LANES, 2 * c, 2 * (c - LANES) + 1)).astype(BF16)
    for g in range(w_ref.shape[3] // grp):
        w = w_ref[0, 0, :, g * grp:(g + 1) * grp].astype(BF16)
        o_ref[0, :, g * grp:(g + 1) * grp] = jnp.dot(w, perm, preferred_element_type=F32).astype(BF16)


def _wcast_kernel(w_ref, o_ref):
    o_ref[0] = w_ref[0, 0].astype(BF16)


def _wprep(w, layer, *, regroup, tr, tn):
    _, ne, r, c = w.shape
    return pl.pallas_call(
        _wprep_kernel if regroup else _wcast_kernel,
        grid=(ne, r // tr, c // tn),
        in_specs=[pl.BlockSpec((1, 1, tr, tn), lambda e, i, j: (layer, e, i, j))],
        out_specs=pl.BlockSpec((1, tr, tn), lambda e, i, j: (e, i, j)),
        out_shape=jax.ShapeDtypeStruct((ne, r, c), BF16),
        compiler_params=_cparams("arbitrary", "arbitrary", "arbitrary"),
        name="wprep" if regroup else "wcast",
    )(w)


def _regroup_gate_up(b):
    lead = b.shape[:-1]
    return b.reshape(lead + (-1, LANES, 2)).swapaxes(-1, -2).reshape(lead + (-1,))


def _moe_kernel(sbe_ref, live_ref, rows_hbm, h_hbm, wgu_ref, bgu_ref, wd_ref, bd_ref, y_hbm,
                rows_smem, xg0, xg1, acc0, acc1, sem_idx, sem_in, sem_out, *, nj):
    s = pl.program_id(0)
    j = pl.program_id(1)
    n_sb = pl.num_programs(0)
    xgs, accs = (xg0, xg1), (acc0, acc1)
    nsub = (live_ref[s] + MOE_SUB - 1) // MOE_SUB
    nsub_prev = jnp.where(s > 0, (live_ref[jnp.maximum(s - 1, 0)] + MOE_SUB - 1) // MOE_SUB, 0)
    grp_rows = MOE_TM // (MOE_SUBS_PER_BLOCK * nj)
    plan_words = 2 * MOE_TM

    def sub_rows(q, base=0):
        return pl.ds(pl.multiple_of(base + q * MOE_SUB, MOE_SUB), MOE_SUB)

    def start_rows(start_row, lo, hi):
        ngrp = (hi - lo) // MOE_ISSUE_UNROLL

        def group(g, c):
            for u in range(MOE_ISSUE_UNROLL):
                start_row(lo + g * MOE_ISSUE_UNROLL + u)
            return c

        def single(row, c):
            start_row(row)
            return c

        lax.fori_loop(0, ngrp, group, 0)
        lax.fori_loop(lo + ngrp * MOE_ISSUE_UNROLL, hi, single, 0)

    def wait_blocks(src, dst, sem, nblocks):
        def block(q, c):
            pltpu.make_async_copy(src.at[pl.ds(0, MOE_SUB), :], dst.at[pl.ds(0, MOE_SUB), :], sem).wait()
            return c

        lax.fori_loop(0, nblocks, block, 0)

    def idx_copy(sb, slot_of=None):
        slot = ((sb if slot_of is None else slot_of) % 4) * plan_words
        return pltpu.make_async_copy(rows_hbm.at[pl.ds(pl.multiple_of(sb * plan_words, plan_words), plan_words)],
                                     rows_smem.at[pl.ds(pl.multiple_of(slot, plan_words), plan_words)], sem_idx)

    def gather_row(sb, buf, sem, row):
        tok = rows_smem[(sb % 4) * plan_words + row]
        pltpu.make_async_copy(h_hbm.at[pl.ds(tok, 1), :], buf.at[pl.ds(row, 1), :], sem).start()

    def scatter_prev_row(buf, sem, row):
        dst = rows_smem[((s + 3) % 4) * plan_words + MOE_TM + row]
        pltpu.make_async_copy(buf.at[pl.ds(row, 1), :], y_hbm.at[pl.ds(dst, 1), :], sem).start()

    def remaining_groups(start_row, nsub_done):
        def per_j(jj, c):
            def per_q(q, c2):
                lo = (jj * MOE_SUBS_PER_BLOCK + q) * grp_rows
                start_rows(start_row, lo, lo + grp_rows)
                return c2

            lax.fori_loop(nsub_done, MOE_SUBS_PER_BLOCK, per_q, 0)
            return c

        lax.fori_loop(0, nj, per_j, 0)

    def run(p):
        xg_cur, xg_nxt = xgs[p], xgs[1 - p]
        acc_cur, acc_prv = accs[p], accs[1 - p]
        gather_cur = functools.partial(gather_row, s, xg_cur, sem_in.at[p])
        gather_nxt = functools.partial(gather_row, s + 1, xg_nxt, sem_in.at[1 - p])
        scatter_prv = functools.partial(scatter_prev_row, acc_prv, sem_out.at[1 - p])

        @pl.when(j == 0)
        def _prologue():
            if p == 0:
                @pl.when(s == 0)
                def _():
                    for sb in range(2):
                        idx_copy(sb).start()
                        idx_copy(sb).wait()
                    idx_copy(n_sb - 1, slot_of=3).start()
                    idx_copy(n_sb - 1, slot_of=3).wait()
                    acc_prv[...] = jnp.zeros_like(acc_prv)
                    fill = pltpu.make_async_copy(acc_prv, y_hbm.at[pl.ds(y_hbm.shape[0] - MOE_TM, MOE_TM), :],
                                                 sem_out.at[0])
                    fill.start()
                    fill.wait()

            @pl.when(jnp.logical_and(s > 0, s + 1 < n_sb))
            def _():
                idx_copy(s + 1).wait()

            @pl.when(s + 2 < n_sb)
            def _():
                idx_copy(s + 2).start()

            @pl.when(nsub > 0)
            def _():
                remaining_groups(gather_cur, nsub_prev)

            started = jnp.where(nsub > 0, MOE_SUBS_PER_BLOCK, nsub_prev)
            wait_blocks(h_hbm, xg_cur, sem_in.at[p], started)

            def init(q, c):
                acc_cur[sub_rows(q), :] = jnp.broadcast_to(bd_ref[0], (MOE_SUB, acc_cur.shape[1]))
                return c

            lax.fori_loop(0, MOE_SUBS_PER_BLOCK, init, 0)

        def compute(q, c):
            lo = pl.multiple_of((j * MOE_SUBS_PER_BLOCK + q) * grp_rows, grp_rows)
            for u in range(grp_rows):
                gather_nxt(lo + u)
            x = xg_cur[sub_rows(q), :].astype(BF16)
            hgu = jnp.dot(x, wgu_ref[0], preferred_element_type=F32) + bgu_ref[0]
            nchunk = hgu.shape[1] // LANES
            gate = jnp.concatenate([hgu[:, b * LANES:(b + 1) * LANES] for b in range(0, nchunk, 2)], axis=-1)
            up = jnp.concatenate([hgu[:, b * LANES:(b + 1) * LANES] for b in range(1, nchunk, 2)], axis=-1)
            gate = jnp.minimum(gate, SWIGLU_LIMIT)
            up = jnp.clip(up, -SWIGLU_LIMIT, SWIGLU_LIMIT)
            act = (up + 1.0) * (gate * jax.nn.sigmoid(SWIGLU_ALPHA * gate))
            for u in range(grp_rows):
                scatter_prv(lo + u)
            acc_cur[sub_rows(q), :] += jnp.dot(act.astype(BF16), wd_ref[0], preferred_element_type=F32)
            return c

        lax.fori_loop(0, nsub, compute, 0)

        @pl.when(j == nj - 1)
        def _epilogue():
            @pl.when(nsub_prev > 0)
            def _():
                remaining_groups(scatter_prv, nsub)

            started = jnp.where(nsub_prev > 0, MOE_SUBS_PER_BLOCK, nsub)
            wait_blocks(acc_prv, y_hbm, sem_out.at[1 - p], started)

    for p in range(2):
        pl.when(s % 2 == p)(functools.partial(run, p))


def _moe(h2, sb_e, live, rows, wgu, bgu, wdn, bdn):
    n, d = h2.shape
    ne, f = wdn.shape[0], wdn.shape[1]
    nj = f // MOE_TF
    n_sb = sb_e.shape[0]

    def ff(s, j, live_ref):
        return jnp.where(live_ref[s] > 0, j, nj - 1)

    grid_spec = pltpu.PrefetchScalarGridSpec(
        num_scalar_prefetch=2,
        grid=(n_sb, nj),
        in_specs=[pl.BlockSpec(memory_space=pl.ANY),
                  pl.BlockSpec(memory_space=pl.ANY),
                  pl.BlockSpec((1, d, 2 * MOE_TF), lambda s, j, e, ns: (e[s], 0, ff(s, j, ns))),
                  pl.BlockSpec((1, 1, 2 * MOE_TF), lambda s, j, e, ns: (e[s], 0, ff(s, j, ns))),
                  pl.BlockSpec((1, MOE_TF, d), lambda s, j, e, ns: (e[s], ff(s, j, ns), 0)),
                  pl.BlockSpec((1, 1, d), lambda s, j, e, ns: (e[s], 0, 0))],
        out_specs=pl.BlockSpec(memory_space=pl.ANY),
        scratch_shapes=[pltpu.SMEM((4 * 2 * MOE_TM,), jnp.int32),
                        pltpu.VMEM((MOE_TM, d), F32),
                        pltpu.VMEM((MOE_TM, d), F32),
                        pltpu.VMEM((MOE_TM, d), F32),
                        pltpu.VMEM((MOE_TM, d), F32),
                        pltpu.SemaphoreType.DMA(()),
                        pltpu.SemaphoreType.DMA((2,)),
                        pltpu.SemaphoreType.DMA((2,))])
    assert MOE_TM % (MOE_SUBS_PER_BLOCK * nj) == 0 and n_sb >= 3
    return pl.pallas_call(
        functools.partial(_moe_kernel, nj=nj),
        grid_spec=grid_spec,
        out_shape=jax.ShapeDtypeStruct((TOP_K * n + MOE_TM, d), F32),
        compiler_params=_cparams("arbitrary", "arbitrary"),
        name="moe",
    )(sb_e, live, rows, h2, wgu, bgu, wdn, bdn)


def _combine_kernel(x_ref, y0_ref, y1_ref, y2_ref, y3_ref, gate_ref, gfin_ref, *o_refs, final, starts):
    g = gate_ref[...]
    x = x_ref[...]
    for k, y_ref in enumerate((y0_ref, y1_ref, y2_ref, y3_ref)):
        x = x + g[:, k:k + 1] * y_ref[...]
    x = _rms(x, gfin_ref[...]) if final else x
    if len(o_refs) == 1:
        o_refs[0][...] = x
        return
    i = pl.program_id(0)
    bounds = list(starts) + [pl.num_programs(0)]
    for k, o_ref in enumerate(o_refs):
        @pl.when(jnp.logical_and(i >= bounds[k], i < bounds[k + 1]))
        def _(o_ref=o_ref):
            o_ref[...] = x


def _combine(x1, y, gates, gfin, *, final, tm, split=None):
    n, d = x1.shape
    nb = n // tm
    y_spec = lambda k: pl.BlockSpec((tm, d), lambda i: (k * nb + i, 0))
    if split is None:
        starts = (0,)
        out_specs = pl.BlockSpec((tm, d), lambda i: (i, 0))
        out_shape = jax.ShapeDtypeStruct((n, d), F32)
    else:
        starts, out_specs, out_shape, s0 = [], [], [], 0
        for nk in split:
            assert nk % tm == 0
            out_specs.append(pl.BlockSpec((tm, d), lambda i, s0=s0, nbk=nk // tm: (jnp.clip(i - s0, 0, nbk - 1), 0)))
            out_shape.append(jax.ShapeDtypeStruct((nk, d), F32))
            starts.append(s0)
            s0 += nk // tm
        starts = tuple(starts)
    return pl.pallas_call(
        functools.partial(_combine_kernel, final=final, starts=starts),
        grid=(nb,),
        in_specs=[pl.BlockSpec((tm, d), lambda i: (i, 0)), y_spec(0), y_spec(1), y_spec(2), y_spec(3),
                  pl.BlockSpec((tm, TOP_K), lambda i: (i, 0)),
                  pl.BlockSpec((1, d), lambda i: (0, 0))],
        out_specs=out_specs,
        out_shape=out_shape,
        compiler_params=_cparams("arbitrary"),
        name="combine",
    )(x1, y, y, y, y, gates, gfin.reshape(1, d))


def _pick(n, *cands):
    for c in cands:
        if n % c == 0:
            return c
    raise ValueError(f"no tile for {n}")


def kernel(x_prompt, x_sample, g_mix, w_in, rpb, conv_w, conv_b, w_rg_a, b_rg_a, w_rg_x, b_rg_x, rg_lambda,
           g_attn_out, g_lru_out, w_out, g_ffn, w_router, b_router, w_gate_up, b_gate_up, w_down, b_down, g_final):
    depth = w_in.shape[0]
    d = x_prompt.shape[-1]
    attn_w = rpb.shape[1] * HEAD_DIM
    lru_w = conv_w.shape[-1]
    ne = w_router.shape[-1]
    f = w_down.shape[2]
    seqs = []
    off = 0
    for xs in (x_prompt, x_sample):
        seqs.append((off, xs.shape[0], xs.shape[1]))
        off += xs.shape[0] * xs.shape[1]
    n = off
    x = jnp.concatenate([x_prompt.reshape(-1, d), x_sample.reshape(-1, d)], axis=0)
    assert attn_w % LANES == 0 and lru_w == attn_w, "column blocks of z assume equal attention / recurrent widths"

    for l in range(depth):
        z = _inproj(x, g_mix[l], w_in[l].astype(BF16), tm=_pick(n, 1024, 512, 256), tn=_pick(w_in.shape[2], 1024, 512, 256))
        bias = _na_bias_table(rpb[l])
        wa, wx = w_rg_a[l].astype(BF16), w_rg_x[l].astype(BF16)
        attn, h = [], []
        for (o, b, t) in seqs:
            attn.append(_na(z, bias, row_off=o, batch=b, t=t, attn_w=attn_w, tq=_pick(t, 1024, 512, 256, 128, 64)))
            lru_args = (z, conv_w[l], conv_b[l], wa, b_rg_a[l], wx, b_rg_x[l], rg_lambda[l])
            if b == SUBLANES and n % t == 0 and o % (b * t) == 0:
                h.append(_lru_batched(*lru_args, row_off=o, t=t, col_blk=3, tt=_pick(t, 128, 64)))
            else:
                h.append(_lru(*lru_args, row_off=o, batch=b, t=t, col_blk=3, tt=_pick(t, 64)))
        x1, h2, top_idx, gates = _outproj(attn, h, z, x, g_attn_out[l], g_lru_out[l], w_out[l].astype(BF16),
                                          g_ffn[l], w_router[l], b_router[l], g_blk=4, tm=_pick(n, 256, 128))
        sb_e, live, rows = _moe_plan(top_idx, ne)
        wgu = _wprep(w_gate_up, l, regroup=True, tr=d, tn=_pick(2 * f, 1024, 512, 256))
        wdn = _wprep(w_down, l, regroup=False, tr=_pick(f, 512, 256), tn=d)
        bgu = _regroup_gate_up(b_gate_up[l]).reshape(ne, 1, 2 * f)
        y = _moe(h2, sb_e, live, rows, wgu, bgu, wdn, b_down[l].reshape(ne, 1, d))
        last = l == depth - 1
        x = _combine(x1, y, gates, g_final, final=last, tm=_pick(n, 256, 128),
                     split=tuple(b * t for (_, b, t) in seqs) if last else None)

    return x[0].reshape(x_prompt.shape), x[1].reshape(x_sample.shape)
```

```python
import functools

import jax
import jax.numpy as jnp
from jax import lax
from jax.experimental import pallas as pl
from jax.experimental.pallas import tpu as pltpu

F32 = jnp.float32
BF16 = jnp.bfloat16

GRID_W = 64
HEAD_DIM = 64
NA_ROWS = 8
NA_COLS = 16
CONV_W = 4
RG_C = 8.0
LRU_BLOCK_W = 128
TOP_K = 4
SWIGLU_LIMIT = 7.0
SWIGLU_ALPHA = 1.702
NORM_EPS = 1e-5
MASK_BIAS = -1e30

LANES = 128
SUBLANES = 8
BF16_SUBLANES = 16
VMEM_LIMIT = 56 * 1024 * 1024

MOE_SUB = 256
MOE_SUBS_PER_BLOCK = 4
MOE_TM = MOE_SUB * MOE_SUBS_PER_BLOCK
MOE_TF = 512
MOE_ISSUE_UNROLL = 4


def _cparams(*sem):
    return pltpu.CompilerParams(dimension_semantics=sem, vmem_limit_bytes=VMEM_LIMIT)


def _rms(x, g):
    return x * lax.rsqrt(jnp.mean(x * x, axis=-1, keepdims=True) + NORM_EPS) * g


def _inproj_kernel(x_ref, g_ref, w_ref, o_ref, xn_ref):
    @pl.when(pl.program_id(1) == 0)
    def _():
        xn_ref[...] = _rms(x_ref[...], g_ref[...]).astype(BF16)

    o_ref[...] = jnp.dot(xn_ref[...], w_ref[...], preferred_element_type=F32).astype(o_ref.dtype)


def _inproj(x, g, w, *, tm, tn):
    n, d = x.shape
    wn = w.shape[1]
    return pl.pallas_call(
        _inproj_kernel,
        grid=(n // tm, wn // tn),
        in_specs=[pl.BlockSpec((tm, d), lambda i, j: (i, 0)),
                  pl.BlockSpec((1, d), lambda i, j: (0, 0)),
                  pl.BlockSpec((d, tn), lambda i, j: (0, j))],
        out_specs=pl.BlockSpec((tm, tn), lambda i, j: (i, j)),
        out_shape=jax.ShapeDtypeStruct((n, wn), BF16),
        scratch_shapes=[pltpu.VMEM((tm, d), BF16)],
        compiler_params=_cparams("arbitrary", "arbitrary"),
        name="inproj",
    )(x, g.reshape(1, d), w)


def _na_bias_table(rpb):
    h = rpb.shape[0]
    c = jnp.arange(GRID_W)
    col_start = jnp.clip(c - NA_COLS // 2, 0, GRID_W - NA_COLS)
    j = jnp.arange(GRID_W)
    inwin = (j[None, :] >= col_start[:, None]) & (j[None, :] < col_start[:, None] + NA_COLS)
    dc = jnp.clip(j[None, :] - c[:, None] + (NA_COLS - 1), 0, 2 * NA_COLS - 2)
    delta = jnp.arange(NA_ROWS)
    kk = jnp.arange(NA_ROWS)
    dr = kk[None, :] - delta[:, None] + (NA_ROWS - 1)
    row_sel = jax.nn.one_hot(dr, 2 * NA_ROWS - 1, dtype=F32)
    col_sel = jax.nn.one_hot(dc, 2 * NA_COLS - 1, dtype=F32)
    t = jnp.einsum("dka,hab,cjb->hdkcj", row_sel, rpb.astype(F32), col_sel, precision=lax.Precision.HIGHEST)
    t = jnp.where(inwin[None, None, None], t, MASK_BIAS)
    t = t.transpose(0, 1, 3, 2, 4).reshape(h // 2, 2, NA_ROWS, GRID_W, NA_ROWS * GRID_W)
    return t.transpose(0, 2, 1, 3, 4).reshape(h // 2, NA_ROWS, 2 * GRID_W, NA_ROWS * GRID_W)


def _na_kernel(q_ref, k_ref, v_ref, b_ref, o_ref, s_scr, p_scr, *, rows, rows_per_tile):
    ti = pl.program_id(2)
    lane = lax.broadcasted_iota(jnp.int32, (GRID_W, 2 * HEAD_DIM), 1)
    first_head = lane < HEAD_DIM
    win = NA_ROWS * GRID_W

    def window_start(rr):
        r = ti * rows_per_tile + rr
        r0 = jnp.clip(r - NA_ROWS // 2, 0, rows - NA_ROWS)
        return r - r0, pl.multiple_of(r0 * GRID_W, GRID_W)

    for rr in range(rows_per_tile):
        delta, ks = window_start(rr)
        q = q_ref[pl.ds(rr * GRID_W, GRID_W), :] * (HEAD_DIM ** -0.5)
        zero = jnp.zeros_like(q)
        qm = jnp.concatenate([jnp.where(first_head, q, zero), jnp.where(first_head, zero, q)], axis=0)
        s = lax.dot_general(qm, k_ref[pl.ds(ks, win), :], (((1,), (1,)), ((), ())), preferred_element_type=F32)
        s_scr[rr] = s + b_ref[0, delta]

    for rr in range(rows_per_tile):
        s = s_scr[rr]
        e = jnp.exp(s - jnp.max(s, axis=-1, keepdims=True))
        p_scr[rr] = (e * (1.0 / jnp.sum(e, axis=-1, keepdims=True))).astype(BF16)

    for rr in range(rows_per_tile):
        _, ks = window_start(rr)
        o = jnp.dot(p_scr[rr], v_ref[pl.ds(ks, win), :], preferred_element_type=F32)
        out = jnp.where(first_head, o[:GRID_W], o[GRID_W:])
        o_ref[pl.ds(rr * GRID_W, GRID_W), :] = out.astype(o_ref.dtype)


def _na(z, bias, *, row_off, batch, t, attn_w, tq):
    rows = t // GRID_W
    assert rows >= NA_ROWS and t % tq == 0 and row_off % t == 0 and tq % GRID_W == 0
    hp = attn_w // LANES
    nq = t // tq
    qoff, koff = row_off // tq, row_off // t
    kern = functools.partial(_na_kernel, rows=rows, rows_per_tile=tq // GRID_W)
    return pl.pallas_call(
        kern,
        grid=(batch, hp, nq),
        in_specs=[pl.BlockSpec((tq, LANES), lambda b, h, i: (qoff + b * nq + i, h)),
                  pl.BlockSpec((t, LANES), lambda b, h, i: (koff + b, hp + h)),
                  pl.BlockSpec((t, LANES), lambda b, h, i: (koff + b, 2 * hp + h)),
                  pl.BlockSpec((1,) + bias.shape[1:], lambda b, h, i: (h, 0, 0, 0))],
        out_specs=pl.BlockSpec((tq, LANES), lambda b, h, i: (b * nq + i, h)),
        out_shape=jax.ShapeDtypeStruct((batch * t, attn_w), BF16),
        scratch_shapes=[pltpu.VMEM((tq // GRID_W, 2 * GRID_W, NA_ROWS * GRID_W), F32),
                        pltpu.VMEM((tq // GRID_W, 2 * GRID_W, NA_ROWS * GRID_W), BF16)],
        compiler_params=_cparams("arbitrary", "arbitrary", "arbitrary"),
        name="na",
    )(z, z, z, bias)


def _softplus(x):
    return jnp.maximum(x, 0.0) + jnp.log(1.0 + jnp.exp(-jnp.abs(x)))


def _sigmoid(x):
    return 0.5 * jnp.tanh(0.5 * x) + 0.5


def _lru_coeffs(u, prev, nxt, ext, refs, *, d, ci, tt, nt, t, store):
    cw_ref, cb_ref, wa_ref, ba_ref, wx_ref, bx_ref, lam_ref = refs
    halo = SUBLANES
    ext[pl.ds(0, halo), :] = jnp.where(ci == 0, 0.0, prev)
    ext[pl.ds(halo, tt), :] = u
    ext[pl.ds(halo + tt, halo), :] = jnp.where(ci == nt - 1, 0.0, nxt)
    c = cb_ref[...]
    for j in range(CONV_W):
        c = c + cw_ref[pl.ds(j, 1), :] * ext[pl.ds(halo - CONV_W // 2 + j, tt), :]
    cbf = c.astype(BF16)

    tpos = ci * tt + lax.broadcasted_iota(jnp.int32, (tt, 1), 0)
    is_first = tpos == jnp.where(d == 0, 0, t - 1)
    sp = _softplus(-lam_ref[0])
    for n in range(wa_ref.shape[1]):
        sl = slice(n * LRU_BLOCK_W, (n + 1) * LRU_BLOCK_W)
        cn = cbf[:, sl]
        r = _sigmoid(jnp.dot(cn, wa_ref[0, n], preferred_element_type=F32) + ba_ref[0, :, sl])
        ig = _sigmoid(jnp.dot(cn, wx_ref[0, n], preferred_element_type=F32) + bx_ref[0, :, sl])
        a = jnp.exp(-RG_C * r * sp[:, sl])
        mult = jnp.where(is_first, 1.0, jnp.sqrt(1.0 - a * a))
        store(n, a, mult * (ig * c[:, sl]))


def _lru_batched_kernel(u_ref, up_ref, un_ref, cw_ref, cb_ref, wa_ref, ba_ref, wx_ref, bx_ref, lam_ref, o_ref,
                        ext, a_scr, b_scr, carry, *, tt, nt, t):
    d = pl.program_id(0)
    i = pl.program_id(1)
    ci = jnp.where(d == 0, i, nt - 1 - i)
    nblk = a_scr.shape[0]
    refs = (cw_ref, cb_ref, wa_ref, ba_ref, wx_ref, bx_ref, lam_ref)

    def per_sequence(b, c0):
        def store(n, a, bb):
            a_scr[n, :, b, :] = a
            b_scr[n, :, b, :] = bb

        _lru_coeffs(u_ref[b].astype(F32), up_ref[b].astype(F32)[BF16_SUBLANES - SUBLANES:, :],
                    un_ref[b].astype(F32)[:SUBLANES, :], ext, refs, d=d, ci=ci, tt=tt, nt=nt, t=t, store=store)
        return c0

    lax.fori_loop(0, SUBLANES, per_sequence, 0)

    @pl.when(i == 0)
    def _():
        carry[...] = jnp.zeros_like(carry)

    def step(jj, hs):
        j = jnp.where(d == 0, jj, tt - 1 - jj)
        new = []
        for n in range(nblk):
            h = a_scr[n, j] * hs[n] + b_scr[n, j]
            b_scr[n, j] = h
            new.append(h)
        return tuple(new)

    hs = lax.fori_loop(0, tt, step, tuple(carry[n] for n in range(nblk)))
    for n in range(nblk):
        carry[n] = hs[n]

    def write(b, c0):
        for n in range(nblk):
            o_ref[0, n, b] = b_scr[n, :, b, :]
        return c0

    lax.fori_loop(0, SUBLANES, write, 0)


def _lru_batched(z, cw, cb, wa, ba, wx, bx, lam, *, row_off, t, col_blk, tt):
    c = cw.shape[1]
    nblk = c // LRU_BLOCK_W
    nt = t // tt
    ntot = z.shape[0]
    nb = SUBLANES
    assert t % tt == 0 and ntot % t == 0 and row_off % (nb * t) == 0 and tt % BF16_SUBLANES == 0
    z3 = z.reshape(ntot // t, t, z.shape[1])
    g = row_off // (nb * t)
    hb = BF16_SUBLANES

    def chunk(dd, i):
        return jnp.where(dd == 0, i, nt - 1 - i)

    kern = functools.partial(_lru_batched_kernel, tt=tt, nt=nt, t=t)
    dir_w = lambda dd, i: (dd, 0, 0, 0)
    dir_v = lambda dd, i: (dd, 0, 0)
    out = pl.pallas_call(
        kern,
        grid=(2, nt),
        in_specs=[pl.BlockSpec((nb, tt, c), lambda dd, i: (g, chunk(dd, i), col_blk)),
                  pl.BlockSpec((nb, hb, c), lambda dd, i: (g, jnp.maximum(chunk(dd, i) * (tt // hb) - 1, 0), col_blk)),
                  pl.BlockSpec((nb, hb, c), lambda dd, i: (g, jnp.minimum((chunk(dd, i) + 1) * (tt // hb), t // hb - 1), col_blk)),
                  pl.BlockSpec((CONV_W, c), lambda dd, i: (0, 0)),
                  pl.BlockSpec((1, c), lambda dd, i: (0, 0)),
                  pl.BlockSpec((1, nblk, LRU_BLOCK_W, LRU_BLOCK_W), dir_w),
                  pl.BlockSpec((1, 1, c), dir_v),
                  pl.BlockSpec((1, nblk, LRU_BLOCK_W, LRU_BLOCK_W), dir_w),
                  pl.BlockSpec((1, 1, c), dir_v),
                  pl.BlockSpec((1, 1, c), dir_v)],
        out_specs=pl.BlockSpec((1, nblk, nb, tt, LRU_BLOCK_W), lambda dd, i: (dd, 0, 0, chunk(dd, i), 0)),
        out_shape=jax.ShapeDtypeStruct((2, nblk, nb, t, LRU_BLOCK_W), F32),
        scratch_shapes=[pltpu.VMEM((tt + 2 * SUBLANES, c), F32),
                        pltpu.VMEM((nblk, tt, nb, LRU_BLOCK_W), F32),
                        pltpu.VMEM((nblk, tt, nb, LRU_BLOCK_W), F32),
                        pltpu.VMEM((nblk, nb, LRU_BLOCK_W), F32)],
        compiler_params=_cparams("arbitrary", "arbitrary"),
        name="lru_batched",
    )(z3, z3, z3, cw, cb.reshape(1, c), wa, ba.reshape(2, 1, c), wx, bx.reshape(2, 1, c), lam.reshape(2, 1, c))
    return out.reshape(2, nblk, nb * t, LRU_BLOCK_W)


def _lru_kernel(u_ref, up_ref, un_ref, cw_ref, cb_ref, wa_ref, ba_ref, wx_ref, bx_ref, lam_ref, o_ref,
                ext, a_scr, b_scr, carry, *, tt, nt, t):
    d = pl.program_id(1)
    i = pl.program_id(2)
    ci = jnp.where(d == 0, i, nt - 1 - i)
    nblk = a_scr.shape[0]
    seg = tt // SUBLANES

    def store(n, a, bb):
        a_scr[n] = a
        b_scr[n] = bb

    _lru_coeffs(u_ref[...].astype(F32), up_ref[...].astype(F32)[BF16_SUBLANES - SUBLANES:, :],
                un_ref[...].astype(F32)[:SUBLANES, :], ext,
                (cw_ref, cb_ref, wa_ref, ba_ref, wx_ref, bx_ref, lam_ref), d=d, ci=ci, tt=tt, nt=nt, t=t, store=store)

    @pl.when(i == 0)
    def _():
        carry[...] = jnp.zeros_like(carry)

    def step_index(jj):
        return jnp.where(d == 0, jj, seg - 1 - jj)

    def load(ref, n, j):
        return ref[n, pl.ds(j, SUBLANES, stride=seg), :]

    def local_step(jj, hp):
        j = step_index(jj)
        hs, ps = hp
        new_h, new_p = [], []
        for n in range(nblk):
            a = load(a_scr, n, j)
            new_h.append(a * hs[n] + load(b_scr, n, j))
            new_p.append(a * ps[n])
        return tuple(new_h), tuple(new_p)

    zeros = tuple(jnp.zeros((SUBLANES, LRU_BLOCK_W), F32) for _ in range(nblk))
    ones = tuple(jnp.ones((SUBLANES, LRU_BLOCK_W), F32) for _ in range(nblk))
    h_end, p_end = lax.fori_loop(0, seg, local_step, (zeros, ones))

    cin, cout = [], []
    for n in range(nblk):
        c0 = carry[pl.ds(n, 1), :]
        fwd, cur = [], c0
        for s in range(SUBLANES):
            fwd.append(cur)
            cur = h_end[n][s:s + 1] + p_end[n][s:s + 1] * cur
        fwd_out = cur
        bwd, cur = [None] * SUBLANES, c0
        for s in reversed(range(SUBLANES)):
            bwd[s] = cur
            cur = h_end[n][s:s + 1] + p_end[n][s:s + 1] * cur
        bwd_out = cur
        cin.append(jnp.where(d == 0, jnp.concatenate(fwd, axis=0), jnp.concatenate(bwd, axis=0)))
        cout.append(jnp.where(d == 0, fwd_out, bwd_out))

    def final_step(jj, hs):
        j = step_index(jj)
        new_h = []
        for n in range(nblk):
            h = load(a_scr, n, j) * hs[n] + load(b_scr, n, j)
            o_ref[0, n, pl.ds(j, SUBLANES, stride=seg), :] = h
            new_h.append(h)
        return tuple(new_h)

    lax.fori_loop(0, seg, final_step, tuple(cin))
    for n in range(nblk):
        carry[pl.ds(n, 1), :] = cout[n]


def _lru(z, cw, cb, wa, ba, wx, bx, lam, *, row_off, batch, t, col_blk, tt):
    c = cw.shape[1]
    nblk = c // LRU_BLOCK_W
    nt = t // tt
    ntot = z.shape[0]
    assert t % tt == 0 and row_off % tt == 0 and tt % (SUBLANES * SUBLANES) == 0
    hb = BF16_SUBLANES
    base = row_off // tt

    def chunk(dd, i):
        return jnp.where(dd == 0, i, nt - 1 - i)

    def u_map(b, dd, i):
        return (base + b * nt + chunk(dd, i), col_blk)

    def prev_map(b, dd, i):
        return (jnp.maximum((row_off + b * t + chunk(dd, i) * tt) // hb - 1, 0), col_blk)

    def next_map(b, dd, i):
        return (jnp.minimum((row_off + b * t + (chunk(dd, i) + 1) * tt) // hb, ntot // hb - 1), col_blk)

    kern = functools.partial(_lru_kernel, tt=tt, nt=nt, t=t)
    dir_w = lambda b, dd, i: (dd, 0, 0, 0)
    dir_v = lambda b, dd, i: (dd, 0, 0)
    return pl.pallas_call(
        kern,
        grid=(batch, 2, nt),
        in_specs=[pl.BlockSpec((tt, c), u_map),
                  pl.BlockSpec((hb, c), prev_map),
                  pl.BlockSpec((hb, c), next_map),
                  pl.BlockSpec((CONV_W, c), lambda b, dd, i: (0, 0)),
                  pl.BlockSpec((1, c), lambda b, dd, i: (0, 0)),
                  pl.BlockSpec((1, nblk, LRU_BLOCK_W, LRU_BLOCK_W), dir_w),
                  pl.BlockSpec((1, 1, c), dir_v),
                  pl.BlockSpec((1, nblk, LRU_BLOCK_W, LRU_BLOCK_W), dir_w),
                  pl.BlockSpec((1, 1, c), dir_v),
                  pl.BlockSpec((1, 1, c), dir_v)],
        out_specs=pl.BlockSpec((1, nblk, tt, LRU_BLOCK_W), lambda b, dd, i: (dd, 0, b * nt + chunk(dd, i), 0)),
        out_shape=jax.ShapeDtypeStruct((2, nblk, batch * t, LRU_BLOCK_W), F32),
        scratch_shapes=[pltpu.VMEM((tt + 2 * SUBLANES, c), F32),
                        pltpu.VMEM((nblk, tt, LRU_BLOCK_W), F32),
                        pltpu.VMEM((nblk, tt, LRU_BLOCK_W), F32),
                        pltpu.VMEM((nblk, LRU_BLOCK_W), F32)],
        compiler_params=_cparams("arbitrary", "arbitrary", "arbitrary"),
        name="lru",
    )(z, z, z, cw, cb.reshape(1, c), wa, ba.reshape(2, 1, c), wx, bx.reshape(2, 1, c), lam.reshape(2, 1, c))


def _gelu_tanh(x):
    return 0.5 * x * (1.0 + jnp.tanh(0.7978845608028654 * (x + 0.044715 * (x * x * x))))


def _outproj_kernel(*refs, starts):
    ng = len(starts)
    groups = [refs[3 * k:3 * k + 3] for k in range(ng)]
    (g_ref, x_ref, ga_ref, gl_ref, wo_ref, gf_ref, wrh_ref, wrl_ref, br_ref,
     x1_ref, h2_ref, idx_ref, gate_ref, x1_prev) = refs[3 * ng:]
    i = pl.program_id(0)
    nblk = groups[0][1].shape[1]

    @pl.when(i == 0)
    def _():
        x1_prev[...] = jnp.zeros_like(x1_prev)

    h2 = _rms(x1_prev[...], gf_ref[...])
    h2_ref[...] = h2
    h2_hi = h2.astype(BF16)
    h2_lo = (h2 - h2_hi.astype(F32)).astype(BF16)
    logits = (jnp.dot(h2_hi, wrh_ref[...], preferred_element_type=F32)
              + jnp.dot(h2_lo, wrh_ref[...], preferred_element_type=F32)
              + jnp.dot(h2_hi, wrl_ref[...], preferred_element_type=F32)) + br_ref[...]
    ne = logits.shape[-1]
    eidx = lax.broadcasted_iota(jnp.int32, logits.shape, 1)
    kidx = lax.broadcasted_iota(jnp.int32, idx_ref.shape, 1)
    vals = jnp.zeros(gate_ref.shape, F32)
    idxs = jnp.zeros(idx_ref.shape, jnp.int32)
    cur = logits
    for k in range(TOP_K):
        m = jnp.max(cur, axis=-1, keepdims=True)
        sel = jnp.min(jnp.where(cur == m, eidx, ne), axis=-1, keepdims=True)
        vals = jnp.where(kidx == k, m, vals)
        idxs = jnp.where(kidx == k, sel, idxs)
        cur = jnp.where(eidx == sel, -jnp.inf, cur)
    e = jnp.exp(vals - vals[:, 0:1])
    gate_ref[...] = e / jnp.sum(e, axis=-1, keepdims=True)
    idx_ref[...] = idxs


    def read(k):
        attn_ref, hf_ref, hb_ref = groups[k]
        return (attn_ref[...].astype(F32),
                jnp.concatenate([hf_ref[0, n] + hb_ref[0, n] for n in range(nblk)], axis=-1))

    attn, h = read(0)
    for k in range(1, ng):
        attn_k, h_k = read(k)
        attn = jnp.where(i >= starts[k], attn_k, attn)
        h = jnp.where(i >= starts[k], h_k, h)
    an = _rms(attn, ga_ref[...])
    rn = _rms(h * _gelu_tanh(g_ref[...].astype(F32)), gl_ref[...])
    mixed = jnp.concatenate([an, rn], axis=-1).astype(BF16)
    x1 = x_ref[...] + jnp.dot(mixed, wo_ref[...], preferred_element_type=F32)
    x1_ref[...] = x1
    x1_prev[...] = x1


def _outproj(attns, hs, z, x, ga, gl, wo, gf, wr, br, *, g_blk, tm):
    n, d = x.shape
    aw = attns[0].shape[1]
    nblk = hs[0].shape[1]
    c = nblk * LRU_BLOCK_W
    ne = wr.shape[1]
    nt = n // tm
    wr_hi = wr.astype(BF16)
    wr_lo = (wr - wr_hi.astype(F32)).astype(BF16)
    const = lambda i: (0, 0)
    cur = lambda i: jnp.minimum(i, nt - 1)
    prv = lambda i: jnp.maximum(i - 1, 0)
    starts, group_specs, group_args = [], [], []
    s0 = 0
    for attn, h in zip(attns, hs):
        nb = attn.shape[0] // tm
        assert attn.shape[0] % tm == 0
        local = lambda i, s0=s0, nb=nb: jnp.clip(i - s0, 0, nb - 1)
        group_specs += [pl.BlockSpec((tm, aw), lambda i, f=local: (f(i), 0)),
                        pl.BlockSpec((1, nblk, tm, LRU_BLOCK_W), lambda i, f=local: (0, 0, f(i), 0)),
                        pl.BlockSpec((1, nblk, tm, LRU_BLOCK_W), lambda i, f=local: (1, 0, f(i), 0))]
        group_args += [attn, h, h]
        starts.append(s0)
        s0 += nb
    assert s0 == nt
    return pl.pallas_call(
        functools.partial(_outproj_kernel, starts=tuple(starts)),
        grid=(nt + 1,),
        in_specs=group_specs + [
                  pl.BlockSpec((tm, c), lambda i: (cur(i), g_blk)),
                  pl.BlockSpec((tm, d), lambda i: (cur(i), 0)),
                  pl.BlockSpec((1, aw), const),
                  pl.BlockSpec((1, c), const),
                  pl.BlockSpec((d, d), const),
                  pl.BlockSpec((1, d), const),
                  pl.BlockSpec((d, ne), const),
                  pl.BlockSpec((d, ne), const),
                  pl.BlockSpec((1, ne), const)],
        out_specs=[pl.BlockSpec((tm, d), lambda i: (cur(i), 0)),
                   pl.BlockSpec((tm, d), lambda i: (prv(i), 0)),
                   pl.BlockSpec((tm, TOP_K), lambda i: (prv(i), 0)),
                   pl.BlockSpec((tm, TOP_K), lambda i: (prv(i), 0))],
        out_shape=[jax.ShapeDtypeStruct((n, d), F32),
                   jax.ShapeDtypeStruct((n, d), F32),
                   jax.ShapeDtypeStruct((n, TOP_K), jnp.int32),
                   jax.ShapeDtypeStruct((n, TOP_K), F32)],
        scratch_shapes=[pltpu.VMEM((tm, d), F32)],
        compiler_params=_cparams("arbitrary"),
        name="outproj",
    )(*group_args, z, x, ga.reshape(1, aw), gl.reshape(1, c), wo, gf.reshape(1, d), wr_hi, wr_lo, br.reshape(1, ne))


def _moe_plan(top_idx, n_experts):
    n = top_idx.shape[0]
    m = n * TOP_K
    n_sb = -(-m // MOE_TM) + n_experts + 1
    flat_e = top_idx.reshape(-1)
    order = jnp.argsort(flat_e).astype(jnp.int32)
    counts = jnp.sum((flat_e[:, None] == jnp.arange(n_experts)[None, :]).astype(jnp.int32), axis=0)
    start = jnp.cumsum(counts) - counts
    sb_per_e = (counts + MOE_TM - 1) // MOE_TM
    sb_end = jnp.cumsum(sb_per_e)
    sb = jnp.arange(n_sb, dtype=jnp.int32)
    sb_e = jnp.minimum(jnp.sum((sb_end[None, :] <= sb[:, None]).astype(jnp.int32), axis=1), n_experts - 1)
    local = sb - (sb_end - sb_per_e)[sb_e]
    valid = jnp.clip(counts[sb_e] - local * MOE_TM, 0, MOE_TM)
    valid = jnp.where(sb < sb_end[-1], valid, 0).astype(jnp.int32)
    r = jnp.arange(MOE_TM, dtype=jnp.int32)
    live = r[None, :] < valid[:, None]
    sorted_pos = jnp.clip(start[sb_e][:, None] + local[:, None] * MOE_TM + r[None, :], 0, m - 1)
    a = order[sorted_pos]
    src = jnp.where(live, a // TOP_K, 0)
    dst = jnp.where(live, (a % TOP_K) * n + a // TOP_K, m + r[None, :])
    rows = jnp.concatenate([src, dst], axis=1).reshape(-1).astype(jnp.int32)
    return sb_e, valid, rows


def _wprep_kernel(w_ref, o_ref):
    grp = 2 * LANES
    r = lax.broadcasted_iota(jnp.int32, (grp, grp), 0)
    c = lax.broadcasted_iota(jnp.int32, (grp, grp), 1)
    perm = (r == jnp.where(c < LANES, 2 * c, 2 * (c - LANES) + 1)).astype(BF16)
    for g in range(w_ref.shape[3] // grp):
        w = w_ref[0, 0, :, g * grp:(g + 1) * grp].astype(BF16)
        o_ref[0, :, g * grp:(g + 1) * grp] = jnp.dot(w, perm, preferred_element_type=F32).astype(BF16)


def _wcast_kernel(w_ref, o_ref):
    o_ref[0] = w_ref[0, 0].astype(BF16)


def _wprep(w, layer, *, regroup, tr, tn):
    _, ne, r, c = w.shape
    return pl.pallas_call(
        _wprep_kernel if regroup else _wcast_kernel,
        grid=(ne, r // tr, c // tn),
        in_specs=[pl.BlockSpec((1, 1, tr, tn), lambda e, i, j: (layer, e, i, j))],
        out_specs=pl.BlockSpec((1, tr, tn), lambda e, i, j: (e, i, j)),
        out_shape=jax.ShapeDtypeStruct((ne, r, c), BF16),
        compiler_params=_cparams("arbitrary", "arbitrary", "arbitrary"),
        name="wprep" if regroup else "wcast",
    )(w)


def _regroup_gate_up(b):
    lead = b.shape[:-1]
    return b.reshape(lead + (-1, LANES, 2)).swapaxes(-1, -2).reshape(lead + (-1,))


def _moe_kernel(sbe_ref, live_ref, rows_hbm, h_hbm, wgu_ref, bgu_ref, wd_ref, bd_ref, y_hbm,
                rows_smem, xg0, xg1, acc0, acc1, sem_idx, sem_in, sem_out, *, nj):
    s = pl.program_id(0)
    j = pl.program_id(1)
    n_sb = pl.num_programs(0)
    xgs, accs = (xg0, xg1), (acc0, acc1)
    nsub = (live_ref[s] + MOE_SUB - 1) // MOE_SUB
    nsub_prev = jnp.where(s > 0, (live_ref[jnp.maximum(s - 1, 0)] + MOE_SUB - 1) // MOE_SUB, 0)
    grp_rows = MOE_TM // (MOE_SUBS_PER_BLOCK * nj)
    plan_words = 2 * MOE_TM

    def sub_rows(q, base=0):
        return pl.ds(pl.multiple_of(base + q * MOE_SUB, MOE_SUB), MOE_SUB)

    def start_rows(start_row, lo, hi):
        ngrp = (hi - lo) // MOE_ISSUE_UNROLL

        def group(g, c):
            for u in range(MOE_ISSUE_UNROLL):
                start_row(lo + g * MOE_ISSUE_UNROLL + u)
            return c

        def single(row, c):
            start_row(row)
            return c

        lax.fori_loop(0, ngrp, group, 0)
        lax.fori_loop(lo + ngrp * MOE_ISSUE_UNROLL, hi, single, 0)

    def wait_blocks(src, dst, sem, nblocks):
        def block(q, c):
            pltpu.make_async_copy(src.at[pl.ds(0, MOE_SUB), :], dst.at[pl.ds(0, MOE_SUB), :], sem).wait()
            return c

        lax.fori_loop(0, nblocks, block, 0)

    def idx_copy(sb, slot_of=None):
        slot = ((sb if slot_of is None else slot_of) % 4) * plan_words
        return pltpu.make_async_copy(rows_hbm.at[pl.ds(pl.multiple_of(sb * plan_words, plan_words), plan_words)],
                                     rows_smem.at[pl.ds(pl.multiple_of(slot, plan_words), plan_words)], sem_idx)

    def gather_row(sb, buf, sem, row):
        tok = rows_smem[(sb % 4) * plan_words + row]
        pltpu.make_async_copy(h_hbm.at[pl.ds(tok, 1), :], buf.at[pl.ds(row, 1), :], sem).start()

    def scatter_prev_row(buf, sem, row, priority=0):
        dst = rows_smem[((s + 3) % 4) * plan_words + MOE_TM + row]
        pltpu.make_async_copy(buf.at[pl.ds(row, 1), :], y_hbm.at[pl.ds(dst, 1), :], sem).start(priority)

    def remaining_groups(start_row, nsub_done):
        def per_j(jj, c):
            def per_q(q, c2):
                lo = (jj * MOE_SUBS_PER_BLOCK + q) * grp_rows
                start_rows(start_row, lo, lo + grp_rows)
                return c2

            lax.fori_loop(nsub_done, MOE_SUBS_PER_BLOCK, per_q, 0)
            return c

        lax.fori_loop(0, nj, per_j, 0)

    def run(p):
        xg_cur, xg_nxt = xgs[p], xgs[1 - p]
        acc_cur, acc_prv = accs[p], accs[1 - p]
        gather_cur = functools.partial(gather_row, s, xg_cur, sem_in.at[p])
        gather_nxt = functools.partial(gather_row, s + 1, xg_nxt, sem_in.at[1 - p])
        scatter_prv = functools.partial(scatter_prev_row, acc_prv, sem_out.at[1 - p])

        @pl.when(j == 0)
        def _prologue():
            if p == 0:
                @pl.when(s == 0)
                def _():
                    for sb in range(2):
                        idx_copy(sb).start()
                        idx_copy(sb).wait()
                    idx_copy(n_sb - 1, slot_of=3).start()
                    idx_copy(n_sb - 1, slot_of=3).wait()
                    acc_prv[...] = jnp.zeros_like(acc_prv)
                    fill = pltpu.make_async_copy(acc_prv, y_hbm.at[pl.ds(y_hbm.shape[0] - MOE_TM, MOE_TM), :],
                                                 sem_out.at[0])
                    fill.start()
                    fill.wait()

            @pl.when(jnp.logical_and(s > 0, s + 1 < n_sb))
            def _():
                idx_copy(s + 1).wait()

            @pl.when(s + 2 < n_sb)
            def _():
                idx_copy(s + 2).start()

            @pl.when(nsub > 0)
            def _():
                remaining_groups(gather_cur, nsub_prev)

            started = jnp.where(nsub > 0, MOE_SUBS_PER_BLOCK, nsub_prev)
            wait_blocks(h_hbm, xg_cur, sem_in.at[p], started)

            def init(q, c):
                acc_cur[sub_rows(q), :] = jnp.broadcast_to(bd_ref[0], (MOE_SUB, acc_cur.shape[1]))
                return c

            lax.fori_loop(0, MOE_SUBS_PER_BLOCK, init, 0)

        def compute(q, c):
            lo = pl.multiple_of((j * MOE_SUBS_PER_BLOCK + q) * grp_rows, grp_rows)
            for u in range(grp_rows):
                gather_nxt(lo + u)
            x = xg_cur[sub_rows(q), :].astype(BF16)
            hgu = jnp.dot(x, wgu_ref[0], preferred_element_type=F32) + bgu_ref[0]
            nchunk = hgu.shape[1] // LANES
            gate = jnp.concatenate([hgu[:, b * LANES:(b + 1) * LANES] for b in range(0, nchunk, 2)], axis=-1)
            up = jnp.concatenate([hgu[:, b * LANES:(b + 1) * LANES] for b in range(1, nchunk, 2)], axis=-1)
            gate = jnp.minimum(gate, SWIGLU_LIMIT)
            up = jnp.clip(up, -SWIGLU_LIMIT, SWIGLU_LIMIT)
            act = (up + 1.0) * (gate * jax.nn.sigmoid(SWIGLU_ALPHA * gate))
            for u in range(grp_rows):
                scatter_prv(lo + u, priority=u % 2)
            acc_cur[sub_rows(q), :] += jnp.dot(act.astype(BF16), wd_ref[0], preferred_element_type=F32)
            return c

        lax.fori_loop(0, nsub, compute, 0)

        @pl.when(j == nj - 1)
        def _epilogue():
            @pl.when(nsub_prev > 0)
            def _():
                remaining_groups(scatter_prv, nsub)

            started = jnp.where(nsub_prev > 0, MOE_SUBS_PER_BLOCK, nsub)
            wait_blocks(acc_prv, y_hbm, sem_out.at[1 - p], started)

    for p in range(2):
        pl.when(s % 2 == p)(functools.partial(run, p))


def _moe(h2, sb_e, live, rows, wgu, bgu, wdn, bdn):
    n, d = h2.shape
    ne, f = wdn.shape[0], wdn.shape[1]
    nj = f // MOE_TF
    n_sb = sb_e.shape[0]

    def ff(s, j, live_ref):
        return jnp.where(live_ref[s] > 0, j, nj - 1)

    grid_spec = pltpu.PrefetchScalarGridSpec(
        num_scalar_prefetch=2,
        grid=(n_sb, nj),
        in_specs=[pl.BlockSpec(memory_space=pl.ANY),
                  pl.BlockSpec(memory_space=pl.ANY),
                  pl.BlockSpec((1, d, 2 * MOE_TF), lambda s, j, e, ns: (e[s], 0, ff(s, j, ns))),
                  pl.BlockSpec((1, 1, 2 * MOE_TF), lambda s, j, e, ns: (e[s], 0, ff(s, j, ns))),
                  pl.BlockSpec((1, MOE_TF, d), lambda s, j, e, ns: (e[s], ff(s, j, ns), 0)),
                  pl.BlockSpec((1, 1, d), lambda s, j, e, ns: (e[s], 0, 0))],
        out_specs=pl.BlockSpec(memory_space=pl.ANY),
        scratch_shapes=[pltpu.SMEM((4 * 2 * MOE_TM,), jnp.int32),
                        pltpu.VMEM((MOE_TM, d), F32),
                        pltpu.VMEM((MOE_TM, d), F32),
                        pltpu.VMEM((MOE_TM, d), F32),
                        pltpu.VMEM((MOE_TM, d), F32),
                        pltpu.SemaphoreType.DMA(()),
                        pltpu.SemaphoreType.DMA((2,)),
                        pltpu.SemaphoreType.DMA((2,))])
    assert MOE_TM % (MOE_SUBS_PER_BLOCK * nj) == 0 and n_sb >= 3
    return pl.pallas_call(
        functools.partial(_moe_kernel, nj=nj),
        grid_spec=grid_spec,
        out_shape=jax.ShapeDtypeStruct((TOP_K * n + MOE_TM, d), F32),
        compiler_params=_cparams("arbitrary", "arbitrary"),
        name="moe",
    )(sb_e, live, rows, h2, wgu, bgu, wdn, bdn)


def _combine_kernel(x_ref, y0_ref, y1_ref, y2_ref, y3_ref, gate_ref, gfin_ref, *o_refs, final, starts):
    g = gate_ref[...]
    x = x_ref[...]
    for k, y_ref in enumerate((y0_ref, y1_ref, y2_ref, y3_ref)):
        x = x + g[:, k:k + 1] * y_ref[...]
    x = _rms(x, gfin_ref[...]) if final else x
    if len(o_refs) == 1:
        o_refs[0][...] = x
        return
    i = pl.program_id(0)
    bounds = list(starts) + [pl.num_programs(0)]
    for k, o_ref in enumerate(o_refs):
        @pl.when(jnp.logical_and(i >= bounds[k], i < bounds[k + 1]))
        def _(o_ref=o_ref):
            o_ref[...] = x


def _combine(x1, y, gates, gfin, *, final, tm, split=None):
    n, d = x1.shape
    nb = n // tm
    y_spec = lambda k: pl.BlockSpec((tm, d), lambda i: (k * nb + i, 0))
    if split is None:
        starts = (0,)
        out_specs = pl.BlockSpec((tm, d), lambda i: (i, 0))
        out_shape = jax.ShapeDtypeStruct((n, d), F32)
    else:
        starts, out_specs, out_shape, s0 = [], [], [], 0
        for nk in split:
            assert nk % tm == 0
            out_specs.append(pl.BlockSpec((tm, d), lambda i, s0=s0, nbk=nk // tm: (jnp.clip(i - s0, 0, nbk - 1), 0)))
            out_shape.append(jax.ShapeDtypeStruct((nk, d), F32))
            starts.append(s0)
            s0 += nk // tm
        starts = tuple(starts)
    return pl.pallas_call(
        functools.partial(_combine_kernel, final=final, starts=starts),
        grid=(nb,),
        in_specs=[pl.BlockSpec((tm, d), lambda i: (i, 0)), y_spec(0), y_spec(1), y_spec(2), y_spec(3),
                  pl.BlockSpec((tm, TOP_K), lambda i: (i, 0)),
                  pl.BlockSpec((1, d), lambda i: (0, 0))],
        out_specs=out_specs,
        out_shape=out_shape,
        compiler_params=_cparams("arbitrary"),
        name="combine",
    )(x1, y, y, y, y, gates, gfin.reshape(1, d))


def _pick(n, *cands):
    for c in cands:
        if n % c == 0:
            return c
    raise ValueError(f"no tile for {n}")


def kernel(x_prompt, x_sample, g_mix, w_in, rpb, conv_w, conv_b, w_rg_a, b_rg_a, w_rg_x, b_rg_x, rg_lambda,
           g_attn_out, g_lru_out, w_out, g_ffn, w_router, b_router, w_gate_up, b_gate_up, w_down, b_down, g_final):
    depth = w_in.shape[0]
    d = x_prompt.shape[-1]
    attn_w = rpb.shape[1] * HEAD_DIM
    lru_w = conv_w.shape[-1]
    ne = w_router.shape[-1]
    f = w_down.shape[2]
    seqs = []
    off = 0
    for xs in (x_prompt, x_sample):
        seqs.append((off, xs.shape[0], xs.shape[1]))
        off += xs.shape[0] * xs.shape[1]
    n = off
    x = jnp.concatenate([x_prompt.reshape(-1, d), x_sample.reshape(-1, d)], axis=0)
    assert attn_w % LANES == 0 and lru_w == attn_w, "column blocks of z assume equal attention / recurrent widths"

    for l in range(depth):
        z = _inproj(x, g_mix[l], w_in[l].astype(BF16), tm=_pick(n, 1024, 512, 256), tn=_pick(w_in.shape[2], 1024, 512, 256))
        bias = _na_bias_table(rpb[l])
        wa, wx = w_rg_a[l].astype(BF16), w_rg_x[l].astype(BF16)
        attn, h = [], []
        for (o, b, t) in seqs:
            attn.append(_na(z, bias, row_off=o, batch=b, t=t, attn_w=attn_w, tq=_pick(t, 1024, 512, 256, 128, 64)))
            lru_args = (z, conv_w[l], conv_b[l], wa, b_rg_a[l], wx, b_rg_x[l], rg_lambda[l])
            if b == SUBLANES and n % t == 0 and o % (b * t) == 0:
                h.append(_lru_batched(*lru_args, row_off=o, t=t, col_blk=3, tt=_pick(t, 128, 64)))
            else:
                h.append(_lru(*lru_args, row_off=o, batch=b, t=t, col_blk=3, tt=_pick(t, 64)))
        x1, h2, top_idx, gates = _outproj(attn, h, z, x, g_attn_out[l], g_lru_out[l], w_out[l].astype(BF16),
                                          g_ffn[l], w_router[l], b_router[l], g_blk=4, tm=_pick(n, 256, 128))
        sb_e, live, rows = _moe_plan(top_idx, ne)
        wgu = _wprep(w_gate_up, l, regroup=True, tr=d, tn=_pick(2 * f, 1024, 512, 256))
        wdn = _wprep(w_down, l, regroup=False, tr=_pick(f, 512, 256), tn=d)
        bgu = _regroup_gate_up(b_gate_up[l]).reshape(ne, 1, 2 * f)
        y = _moe(h2, sb_e, live, rows, wgu, bgu, wdn, b_down[l].reshape(ne, 1, d))
        last = l == depth - 1
        x = _combine(x1, y, gates, g_final, final=last, tm=_pick(n, 256, 128),
                     split=tuple(b * t for (_, b, t) in seqs) if last else None)

    return x[0].reshape(x_prompt.shape), x[1].reshape(x_sample.shape)
```
